```python
import jax, jax.numpy as jnp
from jax import lax
import numpy as np

D_MODEL = 2048
BATCH = 4
SEQ = 4096
DEPTH = 4

HEAD_DIM = 128
A_WIDTH = D_MODEL // 2
A_HEADS = A_WIDTH // HEAD_DIM
DILATED_PATTERNS = ((128, 1), (512, 4), (2048, 16))
ATTN_BLOCK = 64
ROPE_THETA = 10000.0
NEG_INF = -1e30
B_WIDTH = D_MODEL // 2
B_CONV = 3
C_WIDTH = D_MODEL
C_GROUPS = 8
C_CHUNK = 128
AB_IN_WIDTH = 4 * A_WIDTH + 4 * B_WIDTH
SG_IN_WIDTH = 3 * C_WIDTH
N_EVEN = (DEPTH + 1) // 2
N_ODD = DEPTH // 2
EPS = 1e-6

kernel_name = 'hybrid_dilated_attn_shortconv_sgu_adaln'


def rms_norm(x, g):
    xf = x.astype(jnp.float32)
    y = xf * lax.rsqrt(jnp.mean(xf * xf, axis=-1, keepdims=True) + EPS)
    return (y * g.astype(jnp.float32)).astype(x.dtype)


def layer_norm(x, g, b):
    xf = x.astype(jnp.float32)
    mu = jnp.mean(xf, axis=-1, keepdims=True)
    xc = xf - mu
    y = xc * lax.rsqrt(jnp.mean(xc * xc, axis=-1, keepdims=True) + EPS)
    return (y * g.astype(jnp.float32) + b.astype(jnp.float32)).astype(x.dtype)


def ada_modulation(c, w_mod, b_mod):
    m = jax.nn.silu(c) @ w_mod + b_mod
    shift, scale, gate = jnp.split(m, 3, axis=-1)
    return shift[:, None, :], scale[:, None, :], gate[:, None, :]


def rope(t, pos):
    half = t.shape[-1] // 2
    inv = ROPE_THETA ** (-jnp.arange(half, dtype=jnp.float32) / half)
    ang = pos[:, None] * inv[None, :]
    cos = jnp.cos(ang)[None, :, None, :]
    sin = jnp.sin(ang)[None, :, None, :]
    tf = t.astype(jnp.float32)
    t1, t2 = tf[..., :half], tf[..., half:]
    out = jnp.concatenate([t1 * cos - t2 * sin, t2 * cos + t1 * sin], axis=-1)
    return out.astype(t.dtype)


def dilated_window_attention(q, k, v, dilation, radius):
    b, h, s, hd = q.shape
    sub_len = s // dilation
    n_blk = -(-sub_len // ATTN_BLOCK)
    lp = n_blk * ATTN_BLOCK
    pad = lp - sub_len

    def to_sub(t):
        return t.reshape(b, h, sub_len, dilation, hd).transpose(0, 1, 3, 2, 4)

    qs = jnp.pad(to_sub(q), ((0, 0), (0, 0), (0, 0), (0, pad), (0, 0)))
    qb = qs.reshape(b, h, dilation, n_blk, ATTN_BLOCK, hd)
    halo = ((0, 0), (0, 0), (0, 0), (ATTN_BLOCK, pad + ATTN_BLOCK), (0, 0))
    kp = jnp.pad(to_sub(k), halo)
    vp = jnp.pad(to_sub(v), halo)

    def band(t):
        return jnp.concatenate(
            [t[:, :, :, o:o + lp].reshape(b, h, dilation, n_blk, ATTN_BLOCK, hd)
             for o in (0, ATTN_BLOCK, 2 * ATTN_BLOCK)], axis=-2)

    kb, vb = band(kp), band(vp)
    blk = jnp.arange(n_blk)[:, None, None] * ATTN_BLOCK
    q_idx = blk + jnp.arange(ATTN_BLOCK)[None, :, None]
    k_idx = blk - ATTN_BLOCK + jnp.arange(3 * ATTN_BLOCK)[None, None, :]
    valid = (jnp.abs(q_idx - k_idx) <= radius) & (k_idx >= 0) & (k_idx < sub_len)

    scores = jnp.einsum('bhrnqd,bhrnkd->bhrnqk', qb, kb,
                        preferred_element_type=jnp.float32) * (hd ** -0.5)
    scores = jnp.where(valid, scores, NEG_INF)
    m = jnp.max(scores, axis=-1, keepdims=True)
    p = jnp.exp(scores - m)
    den = jnp.sum(p, axis=-1, keepdims=True)
    o = jnp.einsum('bhrnqk,bhrnkd->bhrnqd', p, vb.astype(jnp.float32)) / den
    lse = (m + jnp.log(den))[..., 0]
    o = o.reshape(b, h, dilation, lp, hd)[:, :, :, :sub_len]
    o = o.transpose(0, 1, 3, 2, 4).reshape(b, h, s, hd)
    lse = lse.reshape(b, h, dilation, lp)[..., :sub_len].transpose(0, 1, 3, 2).reshape(b, h, s)
    return o, lse


def dilated_mixture_attention(q, k, v):
    outs, lses = [], []
    for window, dilation in DILATED_PATTERNS:
        o, lse = dilated_window_attention(q, k, v, dilation, window // (2 * dilation))
        outs.append(o)
        lses.append(lse)
    w = jax.nn.softmax(jnp.stack(lses, axis=0), axis=0)
    return jnp.einsum('pbhs,pbhsd->bhsd', w, jnp.stack(outs, axis=0))


def short_conv(u, w):
    return lax.conv_general_dilated(
        u, w[:, None, :].astype(u.dtype), window_strides=(1,), padding=((1, 1),),
        dimension_numbers=('NWC', 'WIO', 'NWC'), feature_group_count=u.shape[-1])


def mixer_ab(h, w_in, conv_w, w_out):
    b, s, _ = h.shape
    proj = h @ w_in
    cuts = np.cumsum([A_WIDTH] * 4 + [B_WIDTH] * 3).tolist()
    q, k, v, z_a, u_b, g_b, g_c, z_b = jnp.split(proj, cuts, axis=-1)
    pos = jnp.arange(s, dtype=jnp.float32)
    q = rope(q.reshape(b, s, A_HEADS, HEAD_DIM), pos).transpose(0, 2, 1, 3)
    k = rope(k.reshape(b, s, A_HEADS, HEAD_DIM), pos).transpose(0, 2, 1, 3)
    v = v.reshape(b, s, A_HEADS, HEAD_DIM).transpose(0, 2, 1, 3)
    attn = dilated_mixture_attention(q, k, v)
    y_a = attn.transpose(0, 2, 1, 3).reshape(b, s, A_WIDTH).astype(h.dtype) * jax.nn.silu(z_a)
    y_b = g_b * short_conv(g_c * u_b, conv_w) * jax.nn.silu(z_b)
    return jnp.concatenate([y_a, y_b], axis=-1) @ w_out


def mixer_sgu(h, w_in, ln_g, ln_b, w_s, b_s, w_out):
    b, s, _ = h.shape
    u, v, z = jnp.split(h @ w_in, 3, axis=-1)
    u = jax.nn.gelu(u)
    v = layer_norm(jax.nn.gelu(v), ln_g, ln_b)
    v = v.reshape(b, s // C_CHUNK, C_CHUNK, C_GROUPS, C_WIDTH // C_GROUPS)
    mixed = jnp.einsum('gts,bnsgc->bntgc', w_s, v) + b_s.T[None, None, :, :, None]
    y = u * mixed.reshape(b, s, C_WIDTH) * jax.nn.silu(z)
    return y @ w_out


def setup_inputs(seed: int = 0) -> dict:
    key = jax.random.key(seed)
    ks = jax.random.split(key, 20)
    D = D_MODEL

    def nrm(k, shape, scale):
        return jax.random.normal(k, shape, jnp.float32) * scale

    return {
        'x': nrm(ks[0], (BATCH, SEQ, D), 1.0),
        'c': nrm(ks[1], (BATCH, D), 1.0),
        'ab_norm_g': 1.0 + nrm(ks[2], (N_EVEN, D), 0.02),
        'ab_w_mod': nrm(ks[3], (N_EVEN, D, 3 * D), 0.5 * D ** -0.5),
        'ab_b_mod': nrm(ks[4], (N_EVEN, 3 * D), 0.01),
        'ab_w_in': nrm(ks[5], (N_EVEN, D, AB_IN_WIDTH), D ** -0.5),
        'ab_conv_w': nrm(ks[6], (N_EVEN, B_CONV, B_WIDTH), B_CONV ** -0.5),
        'ab_w_out': nrm(ks[7], (N_EVEN, A_WIDTH + B_WIDTH, D), (A_WIDTH + B_WIDTH) ** -0.5),
        'sg_norm_g': 1.0 + nrm(ks[8], (N_ODD, D), 0.02),
        'sg_w_mod': nrm(ks[9], (N_ODD, D, 3 * D), 0.5 * D ** -0.5),
        'sg_b_mod': nrm(ks[10], (N_ODD, 3 * D), 0.01),
        'sg_w_in': nrm(ks[11], (N_ODD, D, SG_IN_WIDTH), D ** -0.5),
        'sg_ln_g': 1.0 + nrm(ks[12], (N_ODD, C_WIDTH), 0.02),
        'sg_ln_b': nrm(ks[13], (N_ODD, C_WIDTH), 0.01),
        'sg_w_s': nrm(ks[14], (N_ODD, C_GROUPS, C_CHUNK, C_CHUNK), C_CHUNK ** -0.5),
        'sg_b_s': 1.0 + nrm(ks[15], (N_ODD, C_GROUPS, C_CHUNK), 0.01),
        'sg_w_out': nrm(ks[16], (N_ODD, C_WIDTH, D), C_WIDTH ** -0.5),
        'final_norm_g': 1.0 + nrm(ks[17], (D,), 0.02),
    }


def reference(x, c, ab_norm_g, ab_w_mod, ab_b_mod, ab_w_in, ab_conv_w, ab_w_out,
              sg_norm_g, sg_w_mod, sg_b_mod, sg_w_in, sg_ln_g, sg_ln_b, sg_w_s, sg_b_s,
              sg_w_out, final_norm_g):
    for layer in range(DEPTH):
        i = layer // 2
        if layer % 2 == 0:
            shift, scale, gate = ada_modulation(c, ab_w_mod[i], ab_b_mod[i])
            h = rms_norm(x, ab_norm_g[i]) * (1.0 + scale) + shift
            out = mixer_ab(h, ab_w_in[i], ab_conv_w[i], ab_w_out[i])
        else:
            shift, scale, gate = ada_modulation(c, sg_w_mod[i], sg_b_mod[i])
            h = rms_norm(x, sg_norm_g[i]) * (1.0 + scale) + shift
            out = mixer_sgu(h, sg_w_in[i], sg_ln_g[i], sg_ln_b[i], sg_w_s[i], sg_b_s[i], sg_w_out[i])
        x = x + gate * out
    return rms_norm(x, final_norm_g)
```

```python
import functools

import jax
import jax.numpy as jnp
from jax import lax
from jax.experimental import pallas as pl
from jax.experimental.pallas import tpu as pltpu

F32 = jnp.float32
BF16 = jnp.bfloat16

D_MODEL = 2048
HEAD_DIM = 128
A_WIDTH = 1024
A_HEADS = 8
B_WIDTH = 1024
DILATIONS = (1, 4, 16)
RADIUS = 64
ROPE_THETA = 10000.0
NEG_INF = -1e30
C_WIDTH = 2048
C_GROUPS = 8
C_CHUNK = 128
EPS = 1e-6

Q_TILE = 128
K_WIN = Q_TILE + 2 * RADIUS
VMEM_LIMIT = 56 * 1024 * 1024


def _silu(z):
    return z / (1.0 + jnp.exp(-z))


def _gelu_tanh(x):
    c = 0.7978845608028654
    return x * (0.5 * (1.0 + jnp.tanh(c * (x + 0.044715 * (x * x * x)))))


def _mod_kernel(c_ref, w_ref, b_ref, o_ref):
    a = _silu(c_ref[...]).astype(BF16)
    w = w_ref[...].astype(BF16)
    o_ref[...] = jnp.dot(a, w, preferred_element_type=F32) + b_ref[...]


def _modulation(c_pad, w_mod, b_mod, tn=1024):
    n_layers, d, n = w_mod.shape
    rows = c_pad.shape[0]
    return pl.pallas_call(
        _mod_kernel,
        grid=(n_layers, n // tn),
        in_specs=[
            pl.BlockSpec((rows, d), lambda l, j: (0, 0)),
            pl.BlockSpec((None, d, tn), lambda l, j: (l, 0, j)),
            pl.BlockSpec((None, 1, tn), lambda l, j: (l, 0, j)),
        ],
        out_specs=pl.BlockSpec((None, rows, tn), lambda l, j: (l, 0, j)),
        out_shape=jax.ShapeDtypeStruct((n_layers, rows, n), F32),
        compiler_params=pltpu.CompilerParams(
            dimension_semantics=("arbitrary", "arbitrary"),
            vmem_limit_bytes=VMEM_LIMIT),
        name="modulation",
    )(c_pad, w_mod, b_mod.reshape(n_layers, 1, n))


def _inproj_kernel(x_ref, g_ref, shift_ref, scale_ref, w_ref, cos_ref, sin_ref, o_ref,
                   h_scr, *, rope, row_chunk):
    j = pl.program_id(2)
    tm = x_ref.shape[0]

    @pl.when(j == 0)
    def _():
        g = g_ref[...]
        mult = 1.0 + scale_ref[...]
        shift = shift_ref[...]
        for r0 in range(0, tm, row_chunk):
            xf = x_ref[r0:r0 + row_chunk, :]
            ms = jnp.mean(xf * xf, axis=-1, keepdims=True)
            y = xf * lax.rsqrt(ms + EPS) * g
            h_scr[r0:r0 + row_chunk, :] = (y * mult + shift).astype(BF16)

    acc = jnp.dot(h_scr[...], w_ref[...], preferred_element_type=F32)

    if rope:
        @pl.when(j < 2)
        def _():
            f = jnp.where(j == 0, HEAD_DIM ** -0.5, 1.0).astype(F32)
            cos = cos_ref[...] * f
            sin = sin_ref[...] * f
            for hh in range(acc.shape[1] // HEAD_DIM):
                blk = acc[:, hh * HEAD_DIM:(hh + 1) * HEAD_DIM]
                rot = pltpu.roll(blk, HEAD_DIM // 2, 1)
                o_ref[:, hh * HEAD_DIM:(hh + 1) * HEAD_DIM] = (blk * cos + rot * sin).astype(BF16)

        @pl.when(j >= 2)
        def _():
            o_ref[...] = acc.astype(BF16)
    else:
        o_ref[...] = acc.astype(BF16)


def _in_projection(x, norm_g, shift, scale, w_bf16, cos, sin, *, rope, tm=1024, tn=1024):
    b, s, d = x.shape
    n = w_bf16.shape[1]
    if rope:
        assert tn == A_WIDTH
    kern = functools.partial(_inproj_kernel, rope=rope, row_chunk=256)
    return pl.pallas_call(
        kern,
        grid=(b, s // tm, n // tn),
        in_specs=[
            pl.BlockSpec((None, tm, d), lambda bi, i, j: (bi, i, 0)),
            pl.BlockSpec((1, d), lambda bi, i, j: (0, 0)),
            pl.BlockSpec((None, 1, d), lambda bi, i, j: (bi, 0, 0)),
            pl.BlockSpec((None, 1, d), lambda bi, i, j: (bi, 0, 0)),
            pl.BlockSpec((d, tn), lambda bi, i, j: (0, j)),
            pl.BlockSpec((tm, HEAD_DIM), lambda bi, i, j: (i, 0)),
            pl.BlockSpec((tm, HEAD_DIM), lambda bi, i, j: (i, 0)),
        ],
        out_specs=pl.BlockSpec((None, tm, tn), lambda bi, i, j: (bi, i, j)),
        out_shape=jax.ShapeDtypeStruct((b, s, n), BF16),
        scratch_shapes=[pltpu.VMEM((tm, d), BF16)],
        compiler_params=pltpu.CompilerParams(
            dimension_semantics=("arbitrary", "arbitrary", "arbitrary"),
            vmem_limit_bytes=VMEM_LIMIT),
        name="in_projection_rope" if rope else "in_projection",
    )(x, norm_g.reshape(1, d), shift, scale, w_bf16, cos, sin)


def _attn_conv_kernel(q_ref, k_ref, v_ref, za_ref, ub_ref, gb_ref, gc_ref, zb_ref, cw_ref,
                      ya_ref, yb_ref,
                      q32, k32, v32, qd, kd, vd, res_o, res_l, mask_scr, pbuf):
    s = q_ref.shape[0]

    qi = lax.broadcasted_iota(jnp.int32, (Q_TILE, K_WIN), 0)
    ki = lax.broadcasted_iota(jnp.int32, (Q_TILE, K_WIN), 1)
    for t in range(3):
        mask_scr[t] = (jnp.abs(qi + t * RADIUS - ki) <= RADIUS).astype(F32)

    q32[...] = q_ref[...].astype(F32)
    k32[...] = k_ref[...].astype(F32)
    v32[...] = v_ref[...].astype(F32)

    for p, dil in enumerate(DILATIONS):
        n = s // dil
        tiles_per_seg = n // Q_TILE
        if dil == 1:
            qs, ks, vs = q_ref, k_ref, v_ref
        else:
            for r in range(dil):
                qd[r * n:(r + 1) * n, :] = q32[pl.ds(r, n, stride=dil), :].astype(BF16)
                kd[r * n:(r + 1) * n, :] = k32[pl.ds(r, n, stride=dil), :].astype(BF16)
                vd[r * n:(r + 1) * n, :] = v32[pl.ds(r, n, stride=dil), :].astype(BF16)
            qs, ks, vs = qd, kd, vd

        def tile_body(t, carry, qs=qs, ks=ks, vs=vs, n=n, tiles_per_seg=tiles_per_seg, p=p):
            seg = t // tiles_per_seg
            l0 = (t % tiles_per_seg) * Q_TILE
            kstart = jnp.clip(l0 - RADIUS, 0, n - K_WIN)
            which = (l0 - kstart) // RADIUS
            row0 = pl.multiple_of(seg * n + l0, Q_TILE)
            krow0 = pl.multiple_of(seg * n + kstart, RADIUS)
            qt = qs[pl.ds(row0, Q_TILE), :]
            kt = ks[pl.ds(krow0, K_WIN), :]
            vt = vs[pl.ds(krow0, K_WIN), :]
            sc = lax.dot_general(qt, kt, (((1,), (1,)), ((), ())),
                                 preferred_element_type=F32)
            sc = jnp.where(mask_scr[which] != 0.0, sc, NEG_INF)
            m = jnp.max(sc, axis=-1, keepdims=True)
            e = jnp.exp(sc - m)
            den = jnp.sum(e, axis=-1, keepdims=True)
            o = jnp.dot(e.astype(BF16), vt, preferred_element_type=F32) / den
            res_o[p, pl.ds(row0, Q_TILE), :] = o
            res_l[p, pl.ds(row0, Q_TILE), :] = jnp.broadcast_to(m + jnp.log(den),
                                                                 (Q_TILE, HEAD_DIM))
            return carry

        lax.fori_loop(0, s // Q_TILE, tile_body, 0)

    d4, d16 = DILATIONS[1], DILATIONS[2]
    n4, n16 = s // d4, s // d16
    for r in range(d16):
        nat = pl.ds(r, n16, stride=d16)
        rows4 = pl.ds((r % d4) * n4 + r // d4, n16, stride=d16 // d4)
        rows16 = pl.ds(r * n16, n16)
        l0 = res_l[0, nat, :]
        l1 = res_l[1, rows4, :]
        l2 = res_l[2, rows16, :]
        m = jnp.maximum(jnp.maximum(l0, l1), l2)
        e0 = jnp.exp(l0 - m)
        e1 = jnp.exp(l1 - m)
        e2 = jnp.exp(l2 - m)
        num = e0 * res_o[0, nat, :] + e1 * res_o[1, rows4, :] + e2 * res_o[2, rows16, :]
        q32[nat, :] = num / (e0 + e1 + e2)

    ya_ref[...] = (q32[...] * _silu(za_ref[...].astype(F32))).astype(BF16)

    zeros8 = jnp.zeros((8, HEAD_DIM), F32)
    pbuf[0:8, :] = zeros8
    pbuf[8 + s:16 + s, :] = zeros8
    pbuf[8:8 + s, :] = gc_ref[...].astype(F32) * ub_ref[...].astype(F32)
    w0 = cw_ref[0:1, :]
    w1 = cw_ref[1:2, :]
    w2 = cw_ref[2:3, :]
    chunk = 512
    for c0 in range(0, s, chunk):
        conv = (w0 * pbuf[7 + c0:7 + c0 + chunk, :] + w1 * pbuf[8 + c0:8 + c0 + chunk, :]
                + w2 * pbuf[9 + c0:9 + c0 + chunk, :])
        gate = gb_ref[c0:c0 + chunk, :].astype(F32) * _silu(zb_ref[c0:c0 + chunk, :].astype(F32))
        yb_ref[c0:c0 + chunk, :] = (gate * conv).astype(BF16)


def _attn_conv(proj, conv_w):
    b, s, _ = proj.shape
    nblk = A_WIDTH // HEAD_DIM

    def col(slot):
        return pl.BlockSpec((None, s, HEAD_DIM), lambda bi, h: (bi, 0, slot * nblk + h))

    out_spec = pl.BlockSpec((None, s, HEAD_DIM), lambda bi, h: (bi, 0, h))
    return pl.pallas_call(
        _attn_conv_kernel,
        grid=(b, A_HEADS),
        in_specs=[col(0), col(1), col(2), col(3), col(4), col(5), col(6), col(7),
                  pl.BlockSpec((3, HEAD_DIM), lambda bi, h: (0, h))],
        out_specs=[out_spec, out_spec],
        out_shape=[jax.ShapeDtypeStruct((b, s, A_WIDTH), BF16),
                   jax.ShapeDtypeStruct((b, s, B_WIDTH), BF16)],
        scratch_shapes=[
            pltpu.VMEM((s, HEAD_DIM), F32), pltpu.VMEM((s, HEAD_DIM), F32),
            pltpu.VMEM((s, HEAD_DIM), F32),
            pltpu.VMEM((s, HEAD_DIM), BF16), pltpu.VMEM((s, HEAD_DIM), BF16),
            pltpu.VMEM((s, HEAD_DIM), BF16),
            pltpu.VMEM((3, s, HEAD_DIM), F32), pltpu.VMEM((3, s, HEAD_DIM), F32),
            pltpu.VMEM((3, Q_TILE, K_WIN), F32),
            pltpu.VMEM((s + 16, HEAD_DIM), F32),
        ],
        compiler_params=pltpu.CompilerParams(
            dimension_semantics=("arbitrary", "arbitrary"),
            vmem_limit_bytes=VMEM_LIMIT),
        name="attn_conv",
    )(proj, proj, proj, proj, proj, proj, proj, proj, conv_w)


def _ab_out_kernel(ya_ref, yb_ref, w_ref, x_ref, gate_ref, o_ref):
    out = jnp.dot(ya_ref[...], w_ref[0:A_WIDTH, :], preferred_element_type=F32)
    out = out + jnp.dot(yb_ref[...], w_ref[A_WIDTH:A_WIDTH + B_WIDTH, :],
                        preferred_element_type=F32)
    o_ref[...] = x_ref[...] + gate_ref[...] * out


def _ab_out_projection(ya, yb, w_bf16, x, gate, tm=512):
    b, s, d = x.shape
    return pl.pallas_call(
        _ab_out_kernel,
        grid=(b, s // tm),
        in_specs=[
            pl.BlockSpec((None, tm, A_WIDTH), lambda bi, i: (bi, i, 0)),
            pl.BlockSpec((None, tm, B_WIDTH), lambda bi, i: (bi, i, 0)),
            pl.BlockSpec((A_WIDTH + B_WIDTH, d), lambda bi, i: (0, 0)),
            pl.BlockSpec((None, tm, d), lambda bi, i: (bi, i, 0)),
            pl.BlockSpec((None, 1, d), lambda bi, i: (bi, 0, 0)),
        ],
        out_specs=pl.BlockSpec((None, tm, d), lambda bi, i: (bi, i, 0)),
        out_shape=jax.ShapeDtypeStruct((b, s, d), F32),
        compiler_params=pltpu.CompilerParams(
            dimension_semantics=("arbitrary", "arbitrary"),
            vmem_limit_bytes=VMEM_LIMIT),
        name="ab_out_projection",
    )(ya, yb, w_bf16, x, gate)


def _sgu_kernel(u_ref, v_ref, z_ref, lng_ref, lnb_ref, ws_ref, bs_ref, w_ref, x_ref, gate_ref,
                fg_ref, o_ref, y_scr, *, final):
    tm = u_ref.shape[0]
    gw = C_WIDTH // C_GROUPS
    lng = lng_ref[...]
    lnb = lnb_ref[...]
    for c0 in range(0, tm, C_CHUNK):
        rows = slice(c0, c0 + C_CHUNK)
        v = _gelu_tanh(v_ref[rows, :].astype(F32))
        mu = jnp.mean(v, axis=-1, keepdims=True)
        vc = v - mu
        var = jnp.mean(vc * vc, axis=-1, keepdims=True)
        vn = (vc * lax.rsqrt(var + EPS) * lng + lnb).astype(BF16)
        for g in range(C_GROUPS):
            cols = slice(g * gw, (g + 1) * gw)
            mixed = jnp.dot(ws_ref[g], vn[:, cols], preferred_element_type=F32)
            mixed = mixed + bs_ref[:, g:g + 1]
            u = _gelu_tanh(u_ref[rows, cols].astype(F32))
            zg = _silu(z_ref[rows, cols].astype(F32))
            y_scr[rows, cols] = (u * mixed * zg).astype(BF16)
        out = jnp.dot(y_scr[rows, :], w_ref[...], preferred_element_type=F32)
        xn = x_ref[rows, :] + gate_ref[...] * out
        if final:
            ms = jnp.mean(xn * xn, axis=-1, keepdims=True)
            xn = xn * lax.rsqrt(ms + EPS) * fg_ref[...]
        o_ref[rows, :] = xn


def _sgu(proj, ln_g, ln_b, ws_bf16, bs_t, w_bf16, x, gate, final_g, *, final, tm=256):
    b, s, d = x.shape
    cw = C_WIDTH
    kern = functools.partial(_sgu_kernel, final=final)
    return pl.pallas_call(
        kern,
        grid=(b, s // tm),
        in_specs=[
            pl.BlockSpec((None, tm, cw), lambda bi, i: (bi, i, 0)),
            pl.BlockSpec((None, tm, cw), lambda bi, i: (bi, i, 1)),
            pl.BlockSpec((None, tm, cw), lambda bi, i: (bi, i, 2)),
            pl.BlockSpec((1, cw), lambda bi, i: (0, 0)),
            pl.BlockSpec((1, cw), lambda bi, i: (0, 0)),
            pl.BlockSpec((C_GROUPS, C_CHUNK, C_CHUNK), lambda bi, i: (0, 0, 0)),
            pl.BlockSpec((C_CHUNK, C_GROUPS), lambda bi, i: (0, 0)),
            pl.BlockSpec((cw, d), lambda bi, i: (0, 0)),
            pl.BlockSpec((None, tm, d), lambda bi, i: (bi, i, 0)),
            pl.BlockSpec((None, 1, d), lambda bi, i: (bi, 0, 0)),
            pl.BlockSpec((1, d), lambda bi, i: (0, 0)),
        ],
        out_specs=pl.BlockSpec((None, tm, d), lambda bi, i: (bi, i, 0)),
        out_shape=jax.ShapeDtypeStruct((b, s, d), F32),
        scratch_shapes=[pltpu.VMEM((tm, cw), BF16)],
        compiler_params=pltpu.CompilerParams(
            dimension_semantics=("arbitrary", "arbitrary"),
            vmem_limit_bytes=VMEM_LIMIT),
        name="sgu_final" if final else "sgu",
    )(proj, proj, proj, ln_g.reshape(1, cw), ln_b.reshape(1, cw), ws_bf16, bs_t, w_bf16, x,
      gate, final_g.reshape(1, d))


def _rope_tables(s):
    half = HEAD_DIM // 2
    inv = ROPE_THETA ** (-jnp.arange(half, dtype=F32) / half)
    ang = jnp.arange(s, dtype=F32)[:, None] * inv[None, :]
    cos = jnp.cos(ang)
    sin = jnp.sin(ang)
    return (jnp.concatenate([cos, cos], axis=-1), jnp.concatenate([-sin, sin], axis=-1))


def _split_mod(mod, batch):
    m = mod[:batch].reshape(batch, 1, 3, D_MODEL)
    return m[:, :, 0, :], m[:, :, 1, :], m[:, :, 2, :]


def kernel(x, c, ab_norm_g, ab_w_mod, ab_b_mod, ab_w_in, ab_conv_w, ab_w_out, sg_norm_g, sg_w_mod, sg_b_mod, sg_w_in, sg_ln_g, sg_ln_b, sg_w_s, sg_b_s, sg_w_out, final_norm_g):
    batch, s, _ = x.shape
    depth = ab_w_in.shape[0] + sg_w_in.shape[0]
    c_pad = jnp.pad(c, ((0, 16 - batch), (0, 0)))
    mod_ab = _modulation(c_pad, ab_w_mod, ab_b_mod)
    mod_sg = _modulation(c_pad, sg_w_mod, sg_b_mod)
    cos, sin = _rope_tables(s)

    ab_w_in_b = ab_w_in.astype(BF16)
    ab_w_out_b = ab_w_out.astype(BF16)
    sg_w_in_b = sg_w_in.astype(BF16)
    sg_w_out_b = sg_w_out.astype(BF16)
    sg_w_s_b = sg_w_s.astype(BF16)

    for layer in range(depth):
        i = layer // 2
        if layer % 2 == 0:
            shift, scale, gate = _split_mod(mod_ab[i], batch)
            proj = _in_projection(x, ab_norm_g[i], shift, scale, ab_w_in_b[i], cos, sin, rope=True)
            ya, yb = _attn_conv(proj, ab_conv_w[i])
            x = _ab_out_projection(ya, yb, ab_w_out_b[i], x, gate)
        else:
            shift, scale, gate = _split_mod(mod_sg[i], batch)
            proj = _in_projection(x, sg_norm_g[i], shift, scale, sg_w_in_b[i], cos, sin, rope=False)
            x = _sgu(proj, sg_ln_g[i], sg_ln_b[i], sg_w_s_b[i], sg_b_s[i].T, sg_w_out_b[i], x, gate,
                     final_norm_g, final=(layer == depth - 1))
    return x
```

```python
import functools

import jax
import jax.numpy as jnp
from jax import lax
from jax.experimental import pallas as pl
from jax.experimental.pallas import tpu as pltpu

F32 = jnp.float32
BF16 = jnp.bfloat16

D_MODEL = 2048
HEAD_DIM = 128
A_WIDTH = 1024
A_HEADS = 8
B_WIDTH = 1024
DILATIONS = (1, 4, 16)
RADIUS = 64
ROPE_THETA = 10000.0
NEG_INF = -1e30
C_WIDTH = 2048
C_GROUPS = 8
C_CHUNK = 128
EPS = 1e-6

Q_TILE = 128
K_WIN = Q_TILE + 2 * RADIUS
TILE_GROUP = 16
VMEM_LIMIT = 56 * 1024 * 1024


def _silu(z):
    return z / (1.0 + jnp.exp(-z))


def _gelu_tanh(x):
    c = 0.7978845608028654
    return x * (0.5 * (1.0 + jnp.tanh(c * (x + 0.044715 * (x * x * x)))))


def _mod_kernel(c_ref, w_ref, b_ref, o_ref):
    a = _silu(c_ref[...]).astype(BF16)
    w = w_ref[...].astype(BF16)
    o_ref[...] = jnp.dot(a, w, preferred_element_type=F32) + b_ref[...]


def _modulation(c_pad, w_mod, b_mod, tn=1024):
    n_layers, d, n = w_mod.shape
    rows = c_pad.shape[0]
    return pl.pallas_call(
        _mod_kernel,
        grid=(n_layers, n // tn),
        in_specs=[
            pl.BlockSpec((rows, d), lambda l, j: (0, 0)),
            pl.BlockSpec((None, d, tn), lambda l, j: (l, 0, j)),
            pl.BlockSpec((None, 1, tn), lambda l, j: (l, 0, j)),
        ],
        out_specs=pl.BlockSpec((None, rows, tn), lambda l, j: (l, 0, j)),
        out_shape=jax.ShapeDtypeStruct((n_layers, rows, n), F32),
        compiler_params=pltpu.CompilerParams(
            dimension_semantics=("arbitrary", "arbitrary"),
            vmem_limit_bytes=VMEM_LIMIT),
        name="modulation",
    )(c_pad, w_mod, b_mod.reshape(n_layers, 1, n))


def _inproj_kernel(x_ref, g_ref, shift_ref, scale_ref, w_ref, cos_ref, sin_ref, o_ref,
                   h_scr, *, rope, row_chunk):
    j = pl.program_id(2)
    tm = x_ref.shape[0]

    @pl.when(j == 0)
    def _():
        g = g_ref[...]
        mult = 1.0 + scale_ref[...]
        shift = shift_ref[...]
        for r0 in range(0, tm, row_chunk):
            xf = x_ref[r0:r0 + row_chunk, :]
            ms = jnp.mean(xf * xf, axis=-1, keepdims=True)
            y = xf * lax.rsqrt(ms + EPS) * g
            h_scr[r0:r0 + row_chunk, :] = (y * mult + shift).astype(BF16)

    acc = jnp.dot(h_scr[...], w_ref[...], preferred_element_type=F32)

    if rope:
        @pl.when(j < 2)
        def _():
            f = jnp.where(j == 0, HEAD_DIM ** -0.5, 1.0).astype(F32)
            cos = cos_ref[...] * f
            sin = sin_ref[...] * f
            for hh in range(acc.shape[1] // HEAD_DIM):
                blk = acc[:, hh * HEAD_DIM:(hh + 1) * HEAD_DIM]
                rot = pltpu.roll(blk, HEAD_DIM // 2, 1)
                o_ref[:, hh * HEAD_DIM:(hh + 1) * HEAD_DIM] = (blk * cos + rot * sin).astype(BF16)

        @pl.when(j >= 2)
        def _():
            o_ref[...] = acc.astype(BF16)
    else:
        o_ref[...] = acc.astype(BF16)


def _in_projection(x, norm_g, shift, scale, w_bf16, layer, cos, sin, *, rope, tm=1024, tn=1024):
    b, s, d = x.shape
    n = w_bf16.shape[2]
    if rope:
        assert tn == A_WIDTH
    kern = functools.partial(_inproj_kernel, rope=rope, row_chunk=256)
    return pl.pallas_call(
        kern,
        grid=(b, s // tm, n // tn),
        in_specs=[
            pl.BlockSpec((None, tm, d), lambda bi, i, j: (bi, i, 0)),
            pl.BlockSpec((1, d), lambda bi, i, j: (0, 0)),
            pl.BlockSpec((None, 1, d), lambda bi, i, j: (bi, 0, 0)),
            pl.BlockSpec((None, 1, d), lambda bi, i, j: (bi, 0, 0)),
            pl.BlockSpec((None, d, tn), lambda bi, i, j: (layer, 0, j)),
            pl.BlockSpec((tm, HEAD_DIM), lambda bi, i, j: (i, 0)),
            pl.BlockSpec((tm, HEAD_DIM), lambda bi, i, j: (i, 0)),
        ],
        out_specs=pl.BlockSpec((None, tm, tn), lambda bi, i, j: (bi, i, j)),
        out_shape=jax.ShapeDtypeStruct((b, s, n), BF16),
        scratch_shapes=[pltpu.VMEM((tm, d), BF16)],
        compiler_params=pltpu.CompilerParams(
            dimension_semantics=("arbitrary", "arbitrary", "arbitrary"),
            vmem_limit_bytes=VMEM_LIMIT),
        name="in_projection_rope" if rope else "in_projection",
    )(x, norm_g.reshape(1, d), shift, scale, w_bf16, cos, sin)


def _attn_conv_kernel(q_ref, k_ref, v_ref, za_ref, ub_ref, gb_ref, gc_ref, zb_ref, cw_ref,
                      ya_ref, yb_ref,
                      fa, fb, q4, k4, v4, q16, k16, v16, v1, mask_scr, pbuf):
    s = q_ref.shape[0]
    d4, d16 = DILATIONS[1], DILATIONS[2]
    n4, n16 = s // d4, s // d16
    step = d16 // d4
    chunk = 256
    mchunk = 128

    qi = lax.broadcasted_iota(jnp.int32, (Q_TILE, K_WIN), 0)
    ki = lax.broadcasted_iota(jnp.int32, (Q_TILE, K_WIN), 1)
    for t in range(3):
        mask_scr[t] = jnp.where(jnp.abs(qi + t * RADIUS - ki) <= RADIUS, 0.0, NEG_INF)

    ones = jnp.ones((s, HEAD_DIM), BF16)
    v1[:, HEAD_DIM:2 * HEAD_DIM] = ones
    v4[:, HEAD_DIM:2 * HEAD_DIM] = ones
    v16[:, HEAD_DIM:2 * HEAD_DIM] = ones
    v1[:, 0:HEAD_DIM] = v_ref[...]

    for idx, (src, dst4, dst16) in enumerate(((q_ref, q4, q16), (k_ref, k4, k16),
                                              (v_ref, v4, v16))):
        for c0 in range(0, s, chunk):
            fa[idx, c0:c0 + chunk, :] = src[c0:c0 + chunk, :].astype(F32)
        for r in range(d4):
            for c0 in range(0, n4, chunk):
                part = fa[idx, pl.ds(r + d4 * c0, chunk, stride=d4), :]
                fb[idx, r * n4 + c0:r * n4 + c0 + chunk, :] = part
                dst4[r * n4 + c0:r * n4 + c0 + chunk, 0:HEAD_DIM] = part.astype(BF16)
        for r in range(d4):
            for a in range(step):
                part = fb[idx, pl.ds(r * n4 + a, n16, stride=step), :]
                r16 = r + d4 * a
                dst16[r16 * n16:(r16 + 1) * n16, 0:HEAD_DIM] = part.astype(BF16)

    for p, (dil, qs, ks, vs) in enumerate(((1, q_ref, k_ref, v1), (d4, q4, k4, v4),
                                           (d16, q16, k16, v16))):
        n = s // dil
        tiles_per_seg = n // Q_TILE

        def one_tile(t, qs=qs, ks=ks, vs=vs, n=n, tiles_per_seg=tiles_per_seg, p=p):
            seg = t // tiles_per_seg
            l0 = (t % tiles_per_seg) * Q_TILE
            kstart = jnp.clip(l0 - RADIUS, 0, n - K_WIN)
            which = (l0 - kstart) // RADIUS
            row0 = pl.multiple_of(seg * n + l0, Q_TILE)
            krow0 = pl.multiple_of(seg * n + kstart, RADIUS)
            qt = qs[pl.ds(row0, Q_TILE), :]
            kt = ks[pl.ds(krow0, K_WIN), :]
            vt = vs[pl.ds(krow0, K_WIN), :]
            sc = lax.dot_general(qt, kt, (((1,), (1,)), ((), ())),
                                 preferred_element_type=F32)
            sc = sc + mask_scr[which]
            m = jnp.max(sc, axis=-1, keepdims=True)
            e = jnp.exp(sc - m).astype(BF16)
            ov = jnp.dot(e, vt, preferred_element_type=F32)
            den = ov[:, HEAD_DIM:2 * HEAD_DIM]
            fa[p, pl.ds(row0, Q_TILE), :] = ov[:, 0:HEAD_DIM] / den
            fb[p, pl.ds(row0, Q_TILE), :] = m + jnp.log(den)

        def group_body(gi, carry, one_tile=one_tile):
            for u in range(TILE_GROUP):
                one_tile(gi * TILE_GROUP + u)
            return carry

        lax.fori_loop(0, s // (Q_TILE * TILE_GROUP), group_body, 0)

    for r in range(d4):
        for a in range(step):
            for c0 in range(0, n16, mchunk):
                rows4 = pl.ds(r * n4 + a + step * c0, mchunk, stride=step)
                rows16 = pl.ds((r + d4 * a) * n16 + c0, mchunk)
                l1 = fb[1, rows4, :]
                l2 = fb[2, rows16, :]
                m = jnp.maximum(l1, l2)
                e1 = jnp.exp(l1 - m)
                e2 = jnp.exp(l2 - m)
                fa[1, rows4, :] = e1 * fa[1, rows4, :] + e2 * fa[2, rows16, :]
                fb[1, rows4, :] = m
                pbuf[pl.ds(8 + r * n4 + a + step * c0, mchunk, stride=step), :] = e1 + e2
    for r in range(d4):
        for c0 in range(0, n4, mchunk):
            nat = pl.ds(r + d4 * c0, mchunk, stride=d4)
            rows = pl.ds(r * n4 + c0, mchunk)
            l0 = fb[0, nat, :]
            m12 = fb[1, rows, :]
            m = jnp.maximum(l0, m12)
            e0 = jnp.exp(l0 - m)
            f = jnp.exp(m12 - m)
            num = e0 * fa[0, nat, :] + f * fa[1, rows, :]
            fa[0, nat, :] = num / (e0 + f * pbuf[pl.ds(8 + r * n4 + c0, mchunk), :])

    for c0 in range(0, s, chunk):
        rows = slice(c0, c0 + chunk)
        ya_ref[rows, :] = (fa[0, rows, :] * _silu(za_ref[rows, :].astype(F32))).astype(BF16)

    zeros8 = jnp.zeros((8, HEAD_DIM), F32)
    pbuf[0:8, :] = zeros8
    pbuf[8 + s:16 + s, :] = zeros8
    for c0 in range(0, s, chunk):
        rows = slice(c0, c0 + chunk)
        pbuf[8 + c0:8 + c0 + chunk, :] = gc_ref[rows, :].astype(F32) * ub_ref[rows, :].astype(F32)
    w0 = cw_ref[0:1, :]
    w1 = cw_ref[1:2, :]
    w2 = cw_ref[2:3, :]
    for c0 in range(0, s, chunk):
        conv = (w0 * pbuf[7 + c0:7 + c0 + chunk, :] + w1 * pbuf[8 + c0:8 + c0 + chunk, :]
                + w2 * pbuf[9 + c0:9 + c0 + chunk, :])
        gate = gb_ref[c0:c0 + chunk, :].astype(F32) * _silu(zb_ref[c0:c0 + chunk, :].astype(F32))
        yb_ref[c0:c0 + chunk, :] = (gate * conv).astype(BF16)


def _attn_conv(proj, conv_w):
    b, s, _ = proj.shape
    nblk = A_WIDTH // HEAD_DIM

    def col(slot):
        return pl.BlockSpec((None, s, HEAD_DIM), lambda bi, h: (bi, 0, slot * nblk + h))

    out_spec = pl.BlockSpec((None, s, HEAD_DIM), lambda bi, h: (bi, 0, h))
    return pl.pallas_call(
        _attn_conv_kernel,
        grid=(b, A_HEADS),
        in_specs=[col(0), col(1), col(2), col(3), col(4), col(5), col(6), col(7),
                  pl.BlockSpec((3, HEAD_DIM), lambda bi, h: (0, h))],
        out_specs=[out_spec, out_spec],
        out_shape=[jax.ShapeDtypeStruct((b, s, A_WIDTH), BF16),
                   jax.ShapeDtypeStruct((b, s, B_WIDTH), BF16)],
        scratch_shapes=[
            pltpu.VMEM((3, s, HEAD_DIM), F32), pltpu.VMEM((3, s, HEAD_DIM), F32),
            pltpu.VMEM((s, HEAD_DIM), BF16), pltpu.VMEM((s, HEAD_DIM), BF16),
            pltpu.VMEM((s, 2 * HEAD_DIM), BF16),
            pltpu.VMEM((s, HEAD_DIM), BF16), pltpu.VMEM((s, HEAD_DIM), BF16),
            pltpu.VMEM((s, 2 * HEAD_DIM), BF16),
            pltpu.VMEM((s, 2 * HEAD_DIM), BF16),
            pltpu.VMEM((3, Q_TILE, K_WIN), F32),
            pltpu.VMEM((s + 16, HEAD_DIM), F32),
        ],
        compiler_params=pltpu.CompilerParams(
            dimension_semantics=("arbitrary", "arbitrary"),
            vmem_limit_bytes=VMEM_LIMIT),
        name="attn_conv",
    )(proj, proj, proj, proj, proj, proj, proj, proj, conv_w)


def _ab_out_kernel(ya_ref, yb_ref, w_ref, x_ref, gate_ref, o_ref):
    out = jnp.dot(ya_ref[...], w_ref[0:A_WIDTH, :], preferred_element_type=F32)
    out = out + jnp.dot(yb_ref[...], w_ref[A_WIDTH:A_WIDTH + B_WIDTH, :],
                        preferred_element_type=F32)
    o_ref[...] = x_ref[...] + gate_ref[...] * out


def _ab_out_projection(ya, yb, w_bf16, layer, x, gate, tm=512):
    b, s, d = x.shape
    return pl.pallas_call(
        _ab_out_kernel,
        grid=(b, s // tm),
        in_specs=[
            pl.BlockSpec((None, tm, A_WIDTH), lambda bi, i: (bi, i, 0)),
            pl.BlockSpec((None, tm, B_WIDTH), lambda bi, i: (bi, i, 0)),
            pl.BlockSpec((None, A_WIDTH + B_WIDTH, d), lambda bi, i: (layer, 0, 0)),
            pl.BlockSpec((None, tm, d), lambda bi, i: (bi, i, 0)),
            pl.BlockSpec((None, 1, d), lambda bi, i: (bi, 0, 0)),
        ],
        out_specs=pl.BlockSpec((None, tm, d), lambda bi, i: (bi, i, 0)),
        out_shape=jax.ShapeDtypeStruct((b, s, d), F32),
        compiler_params=pltpu.CompilerParams(
            dimension_semantics=("arbitrary", "arbitrary"),
            vmem_limit_bytes=VMEM_LIMIT),
        name="ab_out_projection",
    )(ya, yb, w_bf16, x, gate)


def _sgu_kernel(u_ref, v_ref, z_ref, lng_ref, lnb_ref, ws_ref, bs_ref, w_ref, x_ref, gate_ref,
                fg_ref, o_ref, y_scr, *, final):
    tm = u_ref.shape[0]
    gw = C_WIDTH // C_GROUPS
    lng = lng_ref[...]
    lnb = lnb_ref[...]
    for c0 in range(0, tm, C_CHUNK):
        rows = slice(c0, c0 + C_CHUNK)
        v = _gelu_tanh(v_ref[rows, :].astype(F32))
        mu = jnp.mean(v, axis=-1, keepdims=True)
        vc = v - mu
        var = jnp.mean(vc * vc, axis=-1, keepdims=True)
        vn = (vc * lax.rsqrt(var + EPS) * lng + lnb).astype(BF16)
        for g in range(C_GROUPS):
            cols = slice(g * gw, (g + 1) * gw)
            mixed = jnp.dot(ws_ref[g], vn[:, cols], preferred_element_type=F32)
            mixed = mixed + bs_ref[:, g:g + 1]
            u = _gelu_tanh(u_ref[rows, cols].astype(F32))
            zg = _silu(z_ref[rows, cols].astype(F32))
            y_scr[rows, cols] = (u * mixed * zg).astype(BF16)
        out = jnp.dot(y_scr[rows, :], w_ref[...], preferred_element_type=F32)
        xn = x_ref[rows, :] + gate_ref[...] * out
        if final:
            ms = jnp.mean(xn * xn, axis=-1, keepdims=True)
            xn = xn * lax.rsqrt(ms + EPS) * fg_ref[...]
        o_ref[rows, :] = xn


def _sgu(proj, ln_g, ln_b, ws_bf16, bs_t, w_bf16, layer, x, gate, final_g, *, final, tm=256):
    b, s, d = x.shape
    cw = C_WIDTH
    kern = functools.partial(_sgu_kernel, final=final)
    return pl.pallas_call(
        kern,
        grid=(b, s // tm),
        in_specs=[
            pl.BlockSpec((None, tm, cw), lambda bi, i: (bi, i, 0)),
            pl.BlockSpec((None, tm, cw), lambda bi, i: (bi, i, 1)),
            pl.BlockSpec((None, tm, cw), lambda bi, i: (bi, i, 2)),
            pl.BlockSpec((1, cw), lambda bi, i: (0, 0)),
            pl.BlockSpec((1, cw), lambda bi, i: (0, 0)),
            pl.BlockSpec((None, C_GROUPS, C_CHUNK, C_CHUNK), lambda bi, i: (layer, 0, 0, 0)),
            pl.BlockSpec((C_CHUNK, C_GROUPS), lambda bi, i: (0, 0)),
            pl.BlockSpec((None, cw, d), lambda bi, i: (layer, 0, 0)),
            pl.BlockSpec((None, tm, d), lambda bi, i: (bi, i, 0)),
            pl.BlockSpec((None, 1, d), lambda bi, i: (bi, 0, 0)),
            pl.BlockSpec((1, d), lambda bi, i: (0, 0)),
        ],
        out_specs=pl.BlockSpec((None, tm, d), lambda bi, i: (bi, i, 0)),
        out_shape=jax.ShapeDtypeStruct((b, s, d), F32),
        scratch_shapes=[pltpu.VMEM((tm, cw), BF16)],
        compiler_params=pltpu.CompilerParams(
            dimension_semantics=("arbitrary", "arbitrary"),
            vmem_limit_bytes=VMEM_LIMIT),
        name="sgu_final" if final else "sgu",
    )(proj, proj, proj, ln_g.reshape(1, cw), ln_b.reshape(1, cw), ws_bf16, bs_t, w_bf16, x,
      gate, final_g.reshape(1, d))


def _rope_tables(s):
    half = HEAD_DIM // 2
    inv = ROPE_THETA ** (-jnp.arange(half, dtype=F32) / half)
    ang = jnp.arange(s, dtype=F32)[:, None] * inv[None, :]
    cos = jnp.cos(ang)
    sin = jnp.sin(ang)
    return (jnp.concatenate([cos, cos], axis=-1), jnp.concatenate([-sin, sin], axis=-1))


def _split_mod(mod, batch):
    m = mod[:batch].reshape(batch, 1, 3, D_MODEL)
    return m[:, :, 0, :], m[:, :, 1, :], m[:, :, 2, :]


def kernel(x, c, ab_norm_g, ab_w_mod, ab_b_mod, ab_w_in, ab_conv_w, ab_w_out, sg_norm_g, sg_w_mod, sg_b_mod, sg_w_in, sg_ln_g, sg_ln_b, sg_w_s, sg_b_s, sg_w_out, final_norm_g):
    batch, s, _ = x.shape
    depth = ab_w_in.shape[0] + sg_w_in.shape[0]
    assert depth % 2 == 0
    c_pad = jnp.pad(c, ((0, 16 - batch), (0, 0)))
    mod_ab = _modulation(c_pad, ab_w_mod, ab_b_mod)
    mod_sg = _modulation(c_pad, sg_w_mod, sg_b_mod)
    cos, sin = _rope_tables(s)

    ab_w_in_b = ab_w_in.astype(BF16)
    ab_w_out_b = ab_w_out.astype(BF16)
    sg_w_in_b = sg_w_in.astype(BF16)
    sg_w_out_b = sg_w_out.astype(BF16)
    sg_w_s_b = sg_w_s.astype(BF16)

    for layer in range(depth):
        i = layer // 2
        if layer % 2 == 0:
            shift, scale, gate = _split_mod(mod_ab[i], batch)
            proj = _in_projection(x, ab_norm_g[i], shift, scale, ab_w_in_b, i, cos, sin, rope=True)
            ya, yb = _attn_conv(proj, ab_conv_w[i])
            x = _ab_out_projection(ya, yb, ab_w_out_b, i, x, gate)
        else:
            shift, scale, gate = _split_mod(mod_sg[i], batch)
            proj = _in_projection(x, sg_norm_g[i], shift, scale, sg_w_in_b, i, cos, sin, rope=False)
            x = _sgu(proj, sg_ln_g[i], sg_ln_b[i], sg_w_s_b, sg_b_s[i].T, sg_w_out_b, i, x, gate,
                     final_norm_g, final=(layer == depth - 1))
    return x
```

```python
import functools

import jax
import jax.numpy as jnp
from jax import lax
from jax.experimental import pallas as pl
from jax.experimental.pallas import tpu as pltpu

F32 = jnp.float32
BF16 = jnp.bfloat16

D_MODEL = 2048
HEAD_DIM = 128
A_WIDTH = 1024
A_HEADS = 8
B_WIDTH = 1024
DILATIONS = (1, 4, 16)
RADIUS = 64
ROPE_THETA = 10000.0
NEG_INF = -1e30
C_WIDTH = 2048
C_GROUPS = 8
C_CHUNK = 128
EPS = 1e-6

Q_TILE = 128
K_WIN = Q_TILE + 2 * RADIUS
TILE_GROUP = 16
VMEM_LIMIT = 56 * 1024 * 1024

AB_EPILOGUES = ("rope_q", "rope_k", "none", "silu", "none", "none", "none", "silu")
SG_EPILOGUES = ("gelu", "gelu", "gelu", "gelu", "silu", "silu")


def _silu(z):
    hz = 0.5 * z
    return hz + hz * jnp.tanh(hz)


def _gelu_tanh(x):
    c = 0.7978845608028654
    return x * (0.5 * (1.0 + jnp.tanh(c * (x + 0.044715 * (x * x * x)))))


def _mod_kernel(c_ref, w_ref, b_ref, o_ref):
    a = _silu(c_ref[...]).astype(BF16)
    w = w_ref[...].astype(BF16)
    o_ref[...] = jnp.dot(a, w, preferred_element_type=F32) + b_ref[...]


def _modulation(c_pad, w_mod, b_mod, tn=1024):
    n_layers, d, n = w_mod.shape
    rows = c_pad.shape[0]
    return pl.pallas_call(
        _mod_kernel,
        grid=(n_layers, n // tn),
        in_specs=[
            pl.BlockSpec((rows, d), lambda l, j: (0, 0)),
            pl.BlockSpec((None, d, tn), lambda l, j: (l, 0, j)),
            pl.BlockSpec((None, 1, tn), lambda l, j: (l, 0, j)),
        ],
        out_specs=pl.BlockSpec((None, rows, tn), lambda l, j: (l, 0, j)),
        out_shape=jax.ShapeDtypeStruct((n_layers, rows, n), F32),
        compiler_params=pltpu.CompilerParams(
            dimension_semantics=("arbitrary", "arbitrary"),
            vmem_limit_bytes=VMEM_LIMIT),
        name="modulation",
    )(c_pad, w_mod, b_mod.reshape(n_layers, 1, n))


def _inproj_kernel(x_ref, g_ref, shift_ref, scale_ref, w_ref, cos_ref, sin_ref, o_ref,
                   h_scr, *, epilogues, row_chunk, out_chunk):
    j = pl.program_id(2)
    tm = x_ref.shape[0]

    @pl.when(j == 0)
    def _():
        gm = g_ref[...] * (1.0 + scale_ref[...])
        shift = shift_ref[...]

        def body(c, carry):
            rows = pl.ds(pl.multiple_of(c * row_chunk, row_chunk), row_chunk)
            xf = x_ref[rows, :]
            ms = jnp.mean(xf * xf, axis=-1, keepdims=True)
            h_scr[rows, :] = (xf * lax.rsqrt(ms + EPS) * gm + shift).astype(BF16)
            return carry

        lax.fori_loop(0, tm // row_chunk, body, 0, unroll=4)

    def rope(acc, f):
        cos = cos_ref[...] * f
        sin = sin_ref[...] * f
        for hh in range(acc.shape[1] // HEAD_DIM):
            cols = slice(hh * HEAD_DIM, (hh + 1) * HEAD_DIM)
            blk = acc[:, cols]
            o_ref[:, cols] = (blk * cos + pltpu.roll(blk, HEAD_DIM // 2, 1) * sin).astype(BF16)

    def pointwise(acc, fn):
        for r0 in range(0, tm, out_chunk):
            o_ref[r0:r0 + out_chunk, :] = fn(acc[r0:r0 + out_chunk, :]).astype(BF16)

    finish = {
        "rope_q": lambda acc: rope(acc, HEAD_DIM ** -0.5),
        "rope_k": lambda acc: rope(acc, 1.0),
        "none": lambda acc: pointwise(acc, lambda t: t),
        "silu": lambda acc: pointwise(acc, _silu),
        "gelu": lambda acc: pointwise(acc, _gelu_tanh),
    }
    for kind in sorted(set(epilogues)):
        tiles = [t for t, e in enumerate(epilogues) if e == kind]
        cond = functools.reduce(jnp.logical_or, [j == t for t in tiles])

        @pl.when(cond)
        def _(kind=kind):
            finish[kind](jnp.dot(h_scr[...], w_ref[...], preferred_element_type=F32))


def _in_projection(x, norm_g, shift, scale, w_bf16, layer, cos, sin, *, epilogues, tm=1024):
    b, s, d = x.shape
    n = w_bf16.shape[2]
    tn = n // len(epilogues)
    kern = functools.partial(_inproj_kernel, epilogues=epilogues, row_chunk=16, out_chunk=256)
    return pl.pallas_call(
        kern,
        grid=(b, s // tm, n // tn),
        in_specs=[
            pl.BlockSpec((None, tm, d), lambda bi, i, j: (bi, i, 0)),
            pl.BlockSpec((1, d), lambda bi, i, j: (0, 0)),
            pl.BlockSpec((None, 1, d), lambda bi, i, j: (bi, 0, 0)),
            pl.BlockSpec((None, 1, d), lambda bi, i, j: (bi, 0, 0)),
            pl.BlockSpec((None, d, tn), lambda bi, i, j: (layer, 0, j)),
            pl.BlockSpec((tm, HEAD_DIM), lambda bi, i, j: (i, 0)),
            pl.BlockSpec((tm, HEAD_DIM), lambda bi, i, j: (i, 0)),
        ],
        out_specs=pl.BlockSpec((None, tm, tn), lambda bi, i, j: (bi, i, j)),
        out_shape=jax.ShapeDtypeStruct((b, s, n), BF16),
        scratch_shapes=[pltpu.VMEM((tm, d), BF16)],
        compiler_params=pltpu.CompilerParams(
            dimension_semantics=("arbitrary", "arbitrary", "arbitrary"),
            vmem_limit_bytes=VMEM_LIMIT),
        name="in_projection_ab" if "rope_q" in epilogues else "in_projection_sg",
    )(x, norm_g.reshape(1, d), shift, scale, w_bf16, cos, sin)


def _attn_conv_kernel(q_ref, k_ref, v_ref, za_ref, ub_ref, gb_ref, gc_ref, zb_ref, cw_ref,
                      ya_ref, yb_ref,
                      fa, fb, q4, k4, v4, q16, k16, v16, v1, mask_scr, pbuf):
    s = q_ref.shape[0]
    d4, d16 = DILATIONS[1], DILATIONS[2]
    n4, n16 = s // d4, s // d16
    step = d16 // d4
    chunk = 256
    mchunk = 128

    qi = lax.broadcasted_iota(jnp.int32, (Q_TILE, K_WIN), 0)
    ki = lax.broadcasted_iota(jnp.int32, (Q_TILE, K_WIN), 1)
    for t in range(3):
        mask_scr[t] = jnp.where(jnp.abs(qi + t * RADIUS - ki) <= RADIUS, 0.0, NEG_INF)

    ones = jnp.ones((s, HEAD_DIM), BF16)
    v1[:, HEAD_DIM:2 * HEAD_DIM] = ones
    v4[:, HEAD_DIM:2 * HEAD_DIM] = ones
    v16[:, HEAD_DIM:2 * HEAD_DIM] = ones
    v1[:, 0:HEAD_DIM] = v_ref[...]

    for idx, (src, dst4, dst16) in enumerate(((q_ref, q4, q16), (k_ref, k4, k16),
                                              (v_ref, v4, v16))):
        for c0 in range(0, s, chunk):
            fa[idx, c0:c0 + chunk, :] = src[c0:c0 + chunk, :].astype(F32)
        for r in range(d4):
            for c0 in range(0, n4, chunk):
                part = fa[idx, pl.ds(r + d4 * c0, chunk, stride=d4), :]
                fb[idx, r * n4 + c0:r * n4 + c0 + chunk, :] = part
                dst4[r * n4 + c0:r * n4 + c0 + chunk, 0:HEAD_DIM] = part.astype(BF16)
        for r in range(d4):
            for a in range(step):
                part = fb[idx, pl.ds(r * n4 + a, n16, stride=step), :]
                r16 = r + d4 * a
                dst16[r16 * n16:(r16 + 1) * n16, 0:HEAD_DIM] = part.astype(BF16)

    for p, (dil, qs, ks, vs) in enumerate(((1, q_ref, k_ref, v1), (d4, q4, k4, v4),
                                           (d16, q16, k16, v16))):
        n = s // dil
        tiles_per_seg = n // Q_TILE

        def one_tile(t, qs=qs, ks=ks, vs=vs, n=n, tiles_per_seg=tiles_per_seg, p=p):
            seg = t // tiles_per_seg
            l0 = (t % tiles_per_seg) * Q_TILE
            kstart = jnp.clip(l0 - RADIUS, 0, n - K_WIN)
            which = (l0 - kstart) // RADIUS
            row0 = pl.multiple_of(seg * n + l0, Q_TILE)
            krow0 = pl.multiple_of(seg * n + kstart, RADIUS)
            qt = qs[pl.ds(row0, Q_TILE), :]
            kt = ks[pl.ds(krow0, K_WIN), :]
            vt = vs[pl.ds(krow0, K_WIN), :]
            sc = lax.dot_general(qt, kt, (((1,), (1,)), ((), ())),
                                 preferred_element_type=F32)
            sc = sc + mask_scr[which]
            m = jnp.max(sc, axis=-1, keepdims=True)
            e = jnp.exp(sc - m).astype(BF16)
            ov = jnp.dot(e, vt, preferred_element_type=F32)
            den = ov[:, HEAD_DIM:2 * HEAD_DIM]
            fa[p, pl.ds(row0, Q_TILE), :] = ov[:, 0:HEAD_DIM] / den
            fb[p, pl.ds(row0, Q_TILE), :] = m + jnp.log(den)

        def group_body(gi, carry, one_tile=one_tile):
            for u in range(TILE_GROUP):
                one_tile(gi * TILE_GROUP + u)
            return carry

        lax.fori_loop(0, s // (Q_TILE * TILE_GROUP), group_body, 0)

    for r in range(d4):
        for a in range(step):
            for c0 in range(0, n16, mchunk):
                rows4 = pl.ds(r * n4 + a + step * c0, mchunk, stride=step)
                rows16 = pl.ds((r + d4 * a) * n16 + c0, mchunk)
                l1 = fb[1, rows4, :]
                l2 = fb[2, rows16, :]
                m = jnp.maximum(l1, l2)
                e1 = jnp.exp(l1 - m)
                e2 = jnp.exp(l2 - m)
                fa[1, rows4, :] = e1 * fa[1, rows4, :] + e2 * fa[2, rows16, :]
                fb[1, rows4, :] = m
                pbuf[pl.ds(8 + r * n4 + a + step * c0, mchunk, stride=step), :] = e1 + e2
    for r in range(d4):
        for c0 in range(0, n4, mchunk):
            nat = pl.ds(r + d4 * c0, mchunk, stride=d4)
            rows = pl.ds(r * n4 + c0, mchunk)
            l0 = fb[0, nat, :]
            m12 = fb[1, rows, :]
            m = jnp.maximum(l0, m12)
            e0 = jnp.exp(l0 - m)
            f = jnp.exp(m12 - m)
            num = e0 * fa[0, nat, :] + f * fa[1, rows, :]
            fa[0, nat, :] = num / (e0 + f * pbuf[pl.ds(8 + r * n4 + c0, mchunk), :])

    for c0 in range(0, s, chunk):
        rows = slice(c0, c0 + chunk)
        ya_ref[rows, :] = (fa[0, rows, :] * za_ref[rows, :].astype(F32)).astype(BF16)

    zeros8 = jnp.zeros((8, HEAD_DIM), F32)
    pbuf[0:8, :] = zeros8
    pbuf[8 + s:16 + s, :] = zeros8
    for c0 in range(0, s, chunk):
        rows = slice(c0, c0 + chunk)
        pbuf[8 + c0:8 + c0 + chunk, :] = gc_ref[rows, :].astype(F32) * ub_ref[rows, :].astype(F32)
    w0 = cw_ref[0:1, :]
    w1 = cw_ref[1:2, :]
    w2 = cw_ref[2:3, :]
    for c0 in range(0, s, chunk):
        conv = (w0 * pbuf[7 + c0:7 + c0 + chunk, :] + w1 * pbuf[8 + c0:8 + c0 + chunk, :]
                + w2 * pbuf[9 + c0:9 + c0 + chunk, :])
        gate = gb_ref[c0:c0 + chunk, :].astype(F32) * zb_ref[c0:c0 + chunk, :].astype(F32)
        yb_ref[c0:c0 + chunk, :] = (gate * conv).astype(BF16)


def _attn_conv(proj, conv_w):
    b, s, _ = proj.shape
    nblk = A_WIDTH // HEAD_DIM

    def col(slot):
        return pl.BlockSpec((None, s, HEAD_DIM), lambda bi, h: (bi, 0, slot * nblk + h))

    out_spec = pl.BlockSpec((None, s, HEAD_DIM), lambda bi, h: (bi, 0, h))
    return pl.pallas_call(
        _attn_conv_kernel,
        grid=(b, A_HEADS),
        in_specs=[col(0), col(1), col(2), col(3), col(4), col(5), col(6), col(7),
                  pl.BlockSpec((3, HEAD_DIM), lambda bi, h: (0, h))],
        out_specs=[out_spec, out_spec],
        out_shape=[jax.ShapeDtypeStruct((b, s, A_WIDTH), BF16),
                   jax.ShapeDtypeStruct((b, s, B_WIDTH), BF16)],
        scratch_shapes=[
            pltpu.VMEM((3, s, HEAD_DIM), F32), pltpu.VMEM((3, s, HEAD_DIM), F32),
            pltpu.VMEM((s, HEAD_DIM), BF16), pltpu.VMEM((s, HEAD_DIM), BF16),
            pltpu.VMEM((s, 2 * HEAD_DIM), BF16),
            pltpu.VMEM((s, HEAD_DIM), BF16), pltpu.VMEM((s, HEAD_DIM), BF16),
            pltpu.VMEM((s, 2 * HEAD_DIM), BF16),
            pltpu.VMEM((s, 2 * HEAD_DIM), BF16),
            pltpu.VMEM((3, Q_TILE, K_WIN), F32),
            pltpu.VMEM((s + 16, HEAD_DIM), F32),
        ],
        compiler_params=pltpu.CompilerParams(
            dimension_semantics=("arbitrary", "arbitrary"),
            vmem_limit_bytes=VMEM_LIMIT),
        name="attn_conv",
    )(proj, proj, proj, proj, proj, proj, proj, proj, conv_w)


def _ab_out_kernel(ya_ref, yb_ref, w_ref, x_ref, gate_ref, o_ref):
    out = jnp.dot(ya_ref[...], w_ref[0:A_WIDTH, :], preferred_element_type=F32)
    out = out + jnp.dot(yb_ref[...], w_ref[A_WIDTH:A_WIDTH + B_WIDTH, :],
                        preferred_element_type=F32)
    o_ref[...] = x_ref[...] + gate_ref[...] * out


def _ab_out_projection(ya, yb, w_bf16, layer, x, gate, tm=512):
    b, s, d = x.shape
    return pl.pallas_call(
        _ab_out_kernel,
        grid=(b, s // tm),
        in_specs=[
            pl.BlockSpec((None, tm, A_WIDTH), lambda bi, i: (bi, i, 0)),
            pl.BlockSpec((None, tm, B_WIDTH), lambda bi, i: (bi, i, 0)),
            pl.BlockSpec((None, A_WIDTH + B_WIDTH, d), lambda bi, i: (layer, 0, 0)),
            pl.BlockSpec((None, tm, d), lambda bi, i: (bi, i, 0)),
            pl.BlockSpec((None, 1, d), lambda bi, i: (bi, 0, 0)),
        ],
        out_specs=pl.BlockSpec((None, tm, d), lambda bi, i: (bi, i, 0)),
        out_shape=jax.ShapeDtypeStruct((b, s, d), F32),
        compiler_params=pltpu.CompilerParams(
            dimension_semantics=("arbitrary", "arbitrary"),
            vmem_limit_bytes=VMEM_LIMIT),
        name="ab_out_projection",
    )(ya, yb, w_bf16, x, gate)


def _sgu_kernel(u_ref, v_ref, z_ref, lng_ref, lnb_ref, ws_ref, bs_ref, w_ref, x_ref, gate_ref,
                fg_ref, o_ref, y_scr, *, final):
    tm = u_ref.shape[0]
    gw = C_WIDTH // C_GROUPS
    lng = lng_ref[...]
    lnb = lnb_ref[...]
    for c0 in range(0, tm, C_CHUNK):
        rows = slice(c0, c0 + C_CHUNK)
        v = v_ref[rows, :].astype(F32)
        mu = jnp.mean(v, axis=-1, keepdims=True)
        vc = v - mu
        var = jnp.mean(vc * vc, axis=-1, keepdims=True)
        vn = (vc * lax.rsqrt(var + EPS) * lng + lnb).astype(BF16)
        for g in range(C_GROUPS):
            cols = slice(g * gw, (g + 1) * gw)
            mixed = jnp.dot(ws_ref[g], vn[:, cols], preferred_element_type=F32)
            mixed = mixed + bs_ref[:, g:g + 1]
            gated = u_ref[rows, cols].astype(F32) * z_ref[rows, cols].astype(F32)
            y_scr[rows, cols] = (gated * mixed).astype(BF16)
        out = jnp.dot(y_scr[rows, :], w_ref[...], preferred_element_type=F32)
        xn = x_ref[rows, :] + gate_ref[...] * out
        if final:
            ms = jnp.mean(xn * xn, axis=-1, keepdims=True)
            xn = xn * lax.rsqrt(ms + EPS) * fg_ref[...]
        o_ref[rows, :] = xn


def _sgu(proj, ln_g, ln_b, ws_bf16, bs_t, w_bf16, layer, x, gate, final_g, *, final, tm=256):
    b, s, d = x.shape
    cw = C_WIDTH
    kern = functools.partial(_sgu_kernel, final=final)
    return pl.pallas_call(
        kern,
        grid=(b, s // tm),
        in_specs=[
            pl.BlockSpec((None, tm, cw), lambda bi, i: (bi, i, 0)),
            pl.BlockSpec((None, tm, cw), lambda bi, i: (bi, i, 1)),
            pl.BlockSpec((None, tm, cw), lambda bi, i: (bi, i, 2)),
            pl.BlockSpec((1, cw), lambda bi, i: (0, 0)),
            pl.BlockSpec((1, cw), lambda bi, i: (0, 0)),
            pl.BlockSpec((None, C_GROUPS, C_CHUNK, C_CHUNK), lambda bi, i: (layer, 0, 0, 0)),
            pl.BlockSpec((C_CHUNK, C_GROUPS), lambda bi, i: (0, 0)),
            pl.BlockSpec((None, cw, d), lambda bi, i: (layer, 0, 0)),
            pl.BlockSpec((None, tm, d), lambda bi, i: (bi, i, 0)),
            pl.BlockSpec((None, 1, d), lambda bi, i: (bi, 0, 0)),
            pl.BlockSpec((1, d), lambda bi, i: (0, 0)),
        ],
        out_specs=pl.BlockSpec((None, tm, d), lambda bi, i: (bi, i, 0)),
        out_shape=jax.ShapeDtypeStruct((b, s, d), F32),
        scratch_shapes=[pltpu.VMEM((tm, cw), BF16)],
        compiler_params=pltpu.CompilerParams(
            dimension_semantics=("arbitrary", "arbitrary"),
            vmem_limit_bytes=VMEM_LIMIT),
        name="sgu_final" if final else "sgu",
    )(proj, proj, proj, ln_g.reshape(1, cw), ln_b.reshape(1, cw), ws_bf16, bs_t, w_bf16, x,
      gate, final_g.reshape(1, d))


def _rope_tables(s):
    half = HEAD_DIM // 2
    inv = ROPE_THETA ** (-jnp.arange(half, dtype=F32) / half)
    ang = jnp.arange(s, dtype=F32)[:, None] * inv[None, :]
    cos = jnp.cos(ang)
    sin = jnp.sin(ang)
    return (jnp.concatenate([cos, cos], axis=-1), jnp.concatenate([-sin, sin], axis=-1))


def _split_mod(mod, batch):
    m = mod[:batch].reshape(batch, 1, 3, D_MODEL)
    return m[:, :, 0, :], m[:, :, 1, :], m[:, :, 2, :]


def kernel(x, c, ab_norm_g, ab_w_mod, ab_b_mod, ab_w_in, ab_conv_w, ab_w_out, sg_norm_g, sg_w_mod, sg_b_mod, sg_w_in, sg_ln_g, sg_ln_b, sg_w_s, sg_b_s, sg_w_out, final_norm_g):
    batch, s, _ = x.shape
    depth = ab_w_in.shape[0] + sg_w_in.shape[0]
    assert depth % 2 == 0
    c_pad = jnp.pad(c, ((0, 16 - batch), (0, 0)))
    mod_ab = _modulation(c_pad, ab_w_mod, ab_b_mod)
    mod_sg = _modulation(c_pad, sg_w_mod, sg_b_mod)
    cos, sin = _rope_tables(s)

    ab_w_in_b = ab_w_in.astype(BF16)
    ab_w_out_b = ab_w_out.astype(BF16)
    sg_w_in_b = sg_w_in.astype(BF16)
    sg_w_out_b = sg_w_out.astype(BF16)
    sg_w_s_b = sg_w_s.astype(BF16)

    for layer in range(depth):
        i = layer // 2
        if layer % 2 == 0:
            shift, scale, gate = _split_mod(mod_ab[i], batch)
            proj = _in_projection(x, ab_norm_g[i], shift, scale, ab_w_in_b, i, cos, sin,
                                  epilogues=AB_EPILOGUES)
            ya, yb = _attn_conv(proj, ab_conv_w[i])
            x = _ab_out_projection(ya, yb, ab_w_out_b, i, x, gate)
        else:
            shift, scale, gate = _split_mod(mod_sg[i], batch)
            proj = _in_projection(x, sg_norm_g[i], shift, scale, sg_w_in_b, i, cos, sin,
                                  epilogues=SG_EPILOGUES)
            x = _sgu(proj, sg_ln_g[i], sg_ln_b[i], sg_w_s_b, sg_b_s[i].T, sg_w_out_b, i, x, gate,
                     final_norm_g, final=(layer == depth - 1))
    return x
```

```python
import functools

import jax
import jax.numpy as jnp
from jax import lax
from jax.experimental import pallas as pl
from jax.experimental.pallas import tpu as pltpu

F32 = jnp.float32
BF16 = jnp.bfloat16

D_MODEL = 2048
HEAD_DIM = 128
A_WIDTH = 1024
A_HEADS = 8
B_WIDTH = 1024
DILATIONS = (1, 4, 16)
RADIUS = 64
ROPE_THETA = 10000.0
NEG_INF = -1e30
C_WIDTH = 2048
C_GROUPS = 8
C_CHUNK = 128
EPS = 1e-6

Q_TILE = 128
K_WIN = Q_TILE + 2 * RADIUS
TILE_GROUP = 16
VMEM_LIMIT = 56 * 1024 * 1024

AB_EPILOGUES = ("rope_q", "rope_k", "none", "silu", "none", "none", "none", "silu")
SG_EPILOGUES = ("gelu", "gelu", "gelu", "gelu", "silu", "silu")


def _silu(z):
    hz = 0.5 * z
    return hz + hz * jnp.tanh(hz)


def _gelu_tanh(x):
    c = 0.7978845608028654
    return x * (0.5 * (1.0 + jnp.tanh(c * (x + 0.044715 * (x * x * x)))))


def _mod_kernel(c_ref, w_ref, b_ref, o_ref):
    a = _silu(c_ref[...]).astype(BF16)
    w = w_ref[...].astype(BF16)
    o_ref[...] = jnp.dot(a, w, preferred_element_type=F32) + b_ref[...]


def _modulation(c_pad, w_mod, b_mod, tn=1024):
    n_layers, d, n = w_mod.shape
    rows = c_pad.shape[0]
    return pl.pallas_call(
        _mod_kernel,
        grid=(n_layers, n // tn),
        in_specs=[
            pl.BlockSpec((rows, d), lambda l, j: (0, 0)),
            pl.BlockSpec((None, d, tn), lambda l, j: (l, 0, j)),
            pl.BlockSpec((None, 1, tn), lambda l, j: (l, 0, j)),
        ],
        out_specs=pl.BlockSpec((None, rows, tn), lambda l, j: (l, 0, j)),
        out_shape=jax.ShapeDtypeStruct((n_layers, rows, n), F32),
        compiler_params=pltpu.CompilerParams(
            dimension_semantics=("arbitrary", "arbitrary"),
            vmem_limit_bytes=VMEM_LIMIT),
        name="modulation",
    )(c_pad, w_mod, b_mod.reshape(n_layers, 1, n))


def _inproj_kernel(x_ref, g_ref, shift_ref, scale_ref, w_ref, cos_ref, sin_ref, o_ref,
                   h_scr, *, epilogues, row_chunk, out_chunk):
    j = pl.program_id(2)
    tm = x_ref.shape[0]

    @pl.when(j == 0)
    def _():
        gm = g_ref[...] * (1.0 + scale_ref[...])
        shift = shift_ref[...]

        def body(c, carry):
            rows = pl.ds(pl.multiple_of(c * row_chunk, row_chunk), row_chunk)
            xf = x_ref[rows, :]
            ms = jnp.mean(xf * xf, axis=-1, keepdims=True)
            h_scr[rows, :] = (xf * lax.rsqrt(ms + EPS) * gm + shift).astype(BF16)
            return carry

        lax.fori_loop(0, tm // row_chunk, body, 0, unroll=4)

    def rope(acc, f):
        cos = cos_ref[...] * f
        sin = sin_ref[...] * f
        for hh in range(acc.shape[1] // HEAD_DIM):
            cols = slice(hh * HEAD_DIM, (hh + 1) * HEAD_DIM)
            blk = acc[:, cols]
            o_ref[:, cols] = (blk * cos + pltpu.roll(blk, HEAD_DIM // 2, 1) * sin).astype(BF16)

    def pointwise(acc, fn):
        for r0 in range(0, tm, out_chunk):
            o_ref[r0:r0 + out_chunk, :] = fn(acc[r0:r0 + out_chunk, :]).astype(BF16)

    finish = {
        "rope_q": lambda acc: rope(acc, HEAD_DIM ** -0.5),
        "rope_k": lambda acc: rope(acc, 1.0),
        "none": lambda acc: pointwise(acc, lambda t: t),
        "silu": lambda acc: pointwise(acc, _silu),
        "gelu": lambda acc: pointwise(acc, _gelu_tanh),
    }
    for kind in sorted(set(epilogues)):
        tiles = [t for t, e in enumerate(epilogues) if e == kind]
        cond = functools.reduce(jnp.logical_or, [j == t for t in tiles])

        @pl.when(cond)
        def _(kind=kind):
            finish[kind](jnp.dot(h_scr[...], w_ref[...], preferred_element_type=F32))


def _in_projection(x, norm_g, shift, scale, w_bf16, layer, cos, sin, *, epilogues, tm=1024):
    b, s, d = x.shape
    n = w_bf16.shape[2]
    tn = n // len(epilogues)
    kern = functools.partial(_inproj_kernel, epilogues=epilogues, row_chunk=16, out_chunk=256)
    return pl.pallas_call(
        kern,
        grid=(b, s // tm, n // tn),
        in_specs=[
            pl.BlockSpec((None, tm, d), lambda bi, i, j: (bi, i, 0)),
            pl.BlockSpec((1, d), lambda bi, i, j: (0, 0)),
            pl.BlockSpec((None, 1, d), lambda bi, i, j: (bi, 0, 0)),
            pl.BlockSpec((None, 1, d), lambda bi, i, j: (bi, 0, 0)),
            pl.BlockSpec((None, d, tn), lambda bi, i, j: (layer, 0, j)),
            pl.BlockSpec((tm, HEAD_DIM), lambda bi, i, j: (i, 0)),
            pl.BlockSpec((tm, HEAD_DIM), lambda bi, i, j: (i, 0)),
        ],
        out_specs=pl.BlockSpec((None, tm, tn), lambda bi, i, j: (bi, i, j)),
        out_shape=jax.ShapeDtypeStruct((b, s, n), BF16),
        scratch_shapes=[pltpu.VMEM((tm, d), BF16)],
        compiler_params=pltpu.CompilerParams(
            dimension_semantics=("arbitrary", "arbitrary", "arbitrary"),
            vmem_limit_bytes=VMEM_LIMIT),
        name="in_projection_ab" if "rope_q" in epilogues else "in_projection_sg",
    )(x, norm_g.reshape(1, d), shift, scale, w_bf16, cos, sin)


def _attn_conv_kernel(q_ref, k_ref, v_ref, za_ref, ub_ref, gb_ref, gc_ref, zb_ref, cw_ref,
                      ya_ref, yb_ref,
                      fa, fb, fc, q4, k4, v4, q16, k16, v16, v1, mask_scr, pbuf):
    s = q_ref.shape[0]
    d4, d16 = DILATIONS[1], DILATIONS[2]
    n4, n16 = s // d4, s // d16
    step = d16 // d4
    chunk = 256
    mchunk = 128

    qi = lax.broadcasted_iota(jnp.int32, (Q_TILE, K_WIN), 0)
    ki = lax.broadcasted_iota(jnp.int32, (Q_TILE, K_WIN), 1)
    for t in range(3):
        mask_scr[t] = jnp.where(jnp.abs(qi + t * RADIUS - ki) <= RADIUS, 0.0, NEG_INF)

    ones = jnp.ones((s, HEAD_DIM), BF16)
    v1[:, HEAD_DIM:2 * HEAD_DIM] = ones
    v4[:, HEAD_DIM:2 * HEAD_DIM] = ones
    v16[:, HEAD_DIM:2 * HEAD_DIM] = ones
    v1[:, 0:HEAD_DIM] = v_ref[...]

    for idx, (src, dst4, dst16) in enumerate(((q_ref, q4, q16), (k_ref, k4, k16),
                                              (v_ref, v4, v16))):
        for c0 in range(0, s, chunk):
            fa[idx, c0:c0 + chunk, :] = src[c0:c0 + chunk, :].astype(F32)
        for r in range(d4):
            for c0 in range(0, n4, chunk):
                part = fa[idx, pl.ds(r + d4 * c0, chunk, stride=d4), :]
                fb[idx, r * n4 + c0:r * n4 + c0 + chunk, :] = part
                dst4[r * n4 + c0:r * n4 + c0 + chunk, 0:HEAD_DIM] = part.astype(BF16)
        for r in range(d4):
            for a in range(step):
                part = fb[idx, pl.ds(r * n4 + a, n16, stride=step), :]
                r16 = r + d4 * a
                dst16[r16 * n16:(r16 + 1) * n16, 0:HEAD_DIM] = part.astype(BF16)

    for p, (dil, qs, ks, vs) in enumerate(((1, q_ref, k_ref, v1), (d4, q4, k4, v4),
                                           (d16, q16, k16, v16))):
        n = s // dil
        tiles_per_seg = n // Q_TILE

        def one_tile(t, qs=qs, ks=ks, vs=vs, n=n, tiles_per_seg=tiles_per_seg, p=p):
            seg = t // tiles_per_seg
            l0 = (t % tiles_per_seg) * Q_TILE
            kstart = jnp.clip(l0 - RADIUS, 0, n - K_WIN)
            which = (l0 - kstart) // RADIUS
            row0 = pl.multiple_of(seg * n + l0, Q_TILE)
            krow0 = pl.multiple_of(seg * n + kstart, RADIUS)
            qt = qs[pl.ds(row0, Q_TILE), :]
            kt = ks[pl.ds(krow0, K_WIN), :]
            vt = vs[pl.ds(krow0, K_WIN), :]
            sc = lax.dot_general(qt, kt, (((1,), (1,)), ((), ())),
                                 preferred_element_type=F32)
            sc = sc + mask_scr[which]
            m = jnp.max(sc, axis=-1, keepdims=True)
            e = jnp.exp(sc - m).astype(BF16)
            ov = jnp.dot(e, vt, preferred_element_type=F32)
            fa[p, pl.ds(row0, Q_TILE), :] = ov[:, 0:HEAD_DIM]
            fb[p, pl.ds(row0, Q_TILE), :] = jnp.broadcast_to(m, (Q_TILE, HEAD_DIM))
            fc[p, pl.ds(row0, Q_TILE), :] = ov[:, HEAD_DIM:2 * HEAD_DIM]

        def group_body(gi, carry, one_tile=one_tile):
            for u in range(TILE_GROUP):
                one_tile(gi * TILE_GROUP + u)
            return carry

        lax.fori_loop(0, s // (Q_TILE * TILE_GROUP), group_body, 0)

    for r in range(d4):
        for a in range(step):
            for c0 in range(0, n16, mchunk):
                rows4 = pl.ds(r * n4 + a + step * c0, mchunk, stride=step)
                rows16 = pl.ds((r + d4 * a) * n16 + c0, mchunk)
                m1 = fb[1, rows4, :]
                m2 = fb[2, rows16, :]
                m = jnp.maximum(m1, m2)
                e1 = jnp.exp(m1 - m)
                e2 = jnp.exp(m2 - m)
                fa[1, rows4, :] = e1 * fa[1, rows4, :] + e2 * fa[2, rows16, :]
                fc[1, rows4, :] = e1 * fc[1, rows4, :] + e2 * fc[2, rows16, :]
                fb[1, rows4, :] = m
    for r in range(d4):
        for c0 in range(0, n4, mchunk):
            nat = pl.ds(r + d4 * c0, mchunk, stride=d4)
            rows = pl.ds(r * n4 + c0, mchunk)
            m0 = fb[0, nat, :]
            m12 = fb[1, rows, :]
            m = jnp.maximum(m0, m12)
            e0 = jnp.exp(m0 - m)
            e12 = jnp.exp(m12 - m)
            num = e0 * fa[0, nat, :] + e12 * fa[1, rows, :]
            den = e0 * fc[0, nat, :] + e12 * fc[1, rows, :]
            fa[0, nat, :] = num / den

    for c0 in range(0, s, chunk):
        rows = slice(c0, c0 + chunk)
        ya_ref[rows, :] = (fa[0, rows, :] * za_ref[rows, :].astype(F32)).astype(BF16)

    zeros8 = jnp.zeros((8, HEAD_DIM), F32)
    pbuf[0:8, :] = zeros8
    pbuf[8 + s:16 + s, :] = zeros8
    for c0 in range(0, s, chunk):
        rows = slice(c0, c0 + chunk)
        pbuf[8 + c0:8 + c0 + chunk, :] = gc_ref[rows, :].astype(F32) * ub_ref[rows, :].astype(F32)
    w0 = cw_ref[0:1, :]
    w1 = cw_ref[1:2, :]
    w2 = cw_ref[2:3, :]
    for c0 in range(0, s, chunk):
        conv = (w0 * pbuf[7 + c0:7 + c0 + chunk, :] + w1 * pbuf[8 + c0:8 + c0 + chunk, :]
                + w2 * pbuf[9 + c0:9 + c0 + chunk, :])
        gate = gb_ref[c0:c0 + chunk, :].astype(F32) * zb_ref[c0:c0 + chunk, :].astype(F32)
        yb_ref[c0:c0 + chunk, :] = (gate * conv).astype(BF16)


def _attn_conv(proj, conv_w):
    b, s, _ = proj.shape
    nblk = A_WIDTH // HEAD_DIM

    def col(slot):
        return pl.BlockSpec((None, s, HEAD_DIM), lambda bi, h: (bi, 0, slot * nblk + h))

    out_spec = pl.BlockSpec((None, s, HEAD_DIM), lambda bi, h: (bi, 0, h))
    return pl.pallas_call(
        _attn_conv_kernel,
        grid=(b, A_HEADS),
        in_specs=[col(0), col(1), col(2), col(3), col(4), col(5), col(6), col(7),
                  pl.BlockSpec((3, HEAD_DIM), lambda bi, h: (0, h))],
        out_specs=[out_spec, out_spec],
        out_shape=[jax.ShapeDtypeStruct((b, s, A_WIDTH), BF16),
                   jax.ShapeDtypeStruct((b, s, B_WIDTH), BF16)],
        scratch_shapes=[
            pltpu.VMEM((3, s, HEAD_DIM), F32), pltpu.VMEM((3, s, HEAD_DIM), F32),
            pltpu.VMEM((3, s, HEAD_DIM), F32),
            pltpu.VMEM((s, HEAD_DIM), BF16), pltpu.VMEM((s, HEAD_DIM), BF16),
            pltpu.VMEM((s, 2 * HEAD_DIM), BF16),
            pltpu.VMEM((s, HEAD_DIM), BF16), pltpu.VMEM((s, HEAD_DIM), BF16),
            pltpu.VMEM((s, 2 * HEAD_DIM), BF16),
            pltpu.VMEM((s, 2 * HEAD_DIM), BF16),
            pltpu.VMEM((3, Q_TILE, K_WIN), F32),
            pltpu.VMEM((s + 16, HEAD_DIM), F32),
        ],
        compiler_params=pltpu.CompilerParams(
            dimension_semantics=("arbitrary", "arbitrary"),
            vmem_limit_bytes=VMEM_LIMIT),
        name="attn_conv",
    )(proj, proj, proj, proj, proj, proj, proj, proj, conv_w)


def _ab_out_kernel(ya_ref, yb_ref, w_ref, x_ref, gate_ref, o_ref):
    out = jnp.dot(ya_ref[...], w_ref[0:A_WIDTH, :], preferred_element_type=F32)
    out = out + jnp.dot(yb_ref[...], w_ref[A_WIDTH:A_WIDTH + B_WIDTH, :],
                        preferred_element_type=F32)
    o_ref[...] = x_ref[...] + gate_ref[...] * out


def _ab_out_projection(ya, yb, w_bf16, layer, x, gate, tm=512):
    b, s, d = x.shape
    return pl.pallas_call(
        _ab_out_kernel,
        grid=(b, s // tm),
        in_specs=[
            pl.BlockSpec((None, tm, A_WIDTH), lambda bi, i: (bi, i, 0)),
            pl.BlockSpec((None, tm, B_WIDTH), lambda bi, i: (bi, i, 0)),
            pl.BlockSpec((None, A_WIDTH + B_WIDTH, d), lambda bi, i: (layer, 0, 0)),
            pl.BlockSpec((None, tm, d), lambda bi, i: (bi, i, 0)),
            pl.BlockSpec((None, 1, d), lambda bi, i: (bi, 0, 0)),
        ],
        out_specs=pl.BlockSpec((None, tm, d), lambda bi, i: (bi, i, 0)),
        out_shape=jax.ShapeDtypeStruct((b, s, d), F32),
        compiler_params=pltpu.CompilerParams(
            dimension_semantics=("arbitrary", "arbitrary"),
            vmem_limit_bytes=VMEM_LIMIT),
        name="ab_out_projection",
    )(ya, yb, w_bf16, x, gate)


def _sgu_kernel(u_ref, v_ref, z_ref, lng_ref, lnb_ref, ws_ref, bs_ref, w_ref, x_ref, gate_ref,
                fg_ref, o_ref, y_scr, *, final, dot_rows):
    tm = u_ref.shape[0]
    gw = C_WIDTH // C_GROUPS
    lng = lng_ref[...]
    lnb = lnb_ref[...]
    for d0 in range(0, tm, dot_rows):
        for c0 in range(d0, d0 + dot_rows, C_CHUNK):
            rows = slice(c0, c0 + C_CHUNK)
            v = v_ref[rows, :].astype(F32)
            mu = jnp.mean(v, axis=-1, keepdims=True)
            vc = v - mu
            var = jnp.mean(vc * vc, axis=-1, keepdims=True)
            vn = (vc * lax.rsqrt(var + EPS) * lng + lnb).astype(BF16)
            for g in range(C_GROUPS):
                cols = slice(g * gw, (g + 1) * gw)
                mixed = jnp.dot(ws_ref[g], vn[:, cols], preferred_element_type=F32)
                mixed = mixed + bs_ref[:, g:g + 1]
                gated = u_ref[rows, cols].astype(F32) * z_ref[rows, cols].astype(F32)
                y_scr[rows, cols] = (gated * mixed).astype(BF16)
        rows = slice(d0, d0 + dot_rows)
        out = jnp.dot(y_scr[rows, :], w_ref[...], preferred_element_type=F32)
        xn = x_ref[rows, :] + gate_ref[...] * out
        if final:
            ms = jnp.mean(xn * xn, axis=-1, keepdims=True)
            xn = xn * lax.rsqrt(ms + EPS) * fg_ref[...]
        o_ref[rows, :] = xn


def _sgu(proj, ln_g, ln_b, ws_bf16, bs_t, w_bf16, layer, x, gate, final_g, *, final, tm=512):
    b, s, d = x.shape
    cw = C_WIDTH
    kern = functools.partial(_sgu_kernel, final=final, dot_rows=256)
    return pl.pallas_call(
        kern,
        grid=(b, s // tm),
        in_specs=[
            pl.BlockSpec((None, tm, cw), lambda bi, i: (bi, i, 0)),
            pl.BlockSpec((None, tm, cw), lambda bi, i: (bi, i, 1)),
            pl.BlockSpec((None, tm, cw), lambda bi, i: (bi, i, 2)),
            pl.BlockSpec((1, cw), lambda bi, i: (0, 0)),
            pl.BlockSpec((1, cw), lambda bi, i: (0, 0)),
            pl.BlockSpec((None, C_GROUPS, C_CHUNK, C_CHUNK), lambda bi, i: (layer, 0, 0, 0)),
            pl.BlockSpec((C_CHUNK, C_GROUPS), lambda bi, i: (0, 0)),
            pl.BlockSpec((None, cw, d), lambda bi, i: (layer, 0, 0), pipeline_mode=pl.Buffered(1)),
            pl.BlockSpec((None, tm, d), lambda bi, i: (bi, i, 0)),
            pl.BlockSpec((None, 1, d), lambda bi, i: (bi, 0, 0)),
            pl.BlockSpec((1, d), lambda bi, i: (0, 0)),
        ],
        out_specs=pl.BlockSpec((None, tm, d), lambda bi, i: (bi, i, 0)),
        out_shape=jax.ShapeDtypeStruct((b, s, d), F32),
        scratch_shapes=[pltpu.VMEM((tm, cw), BF16)],
        compiler_params=pltpu.CompilerParams(
            dimension_semantics=("arbitrary", "arbitrary"),
            vmem_limit_bytes=VMEM_LIMIT),
        name="sgu_final" if final else "sgu",
    )(proj, proj, proj, ln_g.reshape(1, cw), ln_b.reshape(1, cw), ws_bf16, bs_t, w_bf16, x,
      gate, final_g.reshape(1, d))


def _rope_tables(s):
    half = HEAD_DIM // 2
    inv = ROPE_THETA ** (-jnp.arange(half, dtype=F32) / half)
    ang = jnp.arange(s, dtype=F32)[:, None] * inv[None, :]
    cos = jnp.cos(ang)
    sin = jnp.sin(ang)
    return (jnp.concatenate([cos, cos], axis=-1), jnp.concatenate([-sin, sin], axis=-1))


def _split_mod(mod, batch):
    m = mod[:batch].reshape(batch, 1, 3, D_MODEL)
    return m[:, :, 0, :], m[:, :, 1, :], m[:, :, 2, :]


def kernel(x, c, ab_norm_g, ab_w_mod, ab_b_mod, ab_w_in, ab_conv_w, ab_w_out, sg_norm_g, sg_w_mod, sg_b_mod, sg_w_in, sg_ln_g, sg_ln_b, sg_w_s, sg_b_s, sg_w_out, final_norm_g):
    batch, s, _ = x.shape
    depth = ab_w_in.shape[0] + sg_w_in.shape[0]
    assert depth % 2 == 0
    c_pad = jnp.pad(c, ((0, 16 - batch), (0, 0)))
    mod_ab = _modulation(c_pad, ab_w_mod, ab_b_mod)
    mod_sg = _modulation(c_pad, sg_w_mod, sg_b_mod)
    cos, sin = _rope_tables(s)

    ab_w_in_b = ab_w_in.astype(BF16)
    ab_w_out_b = ab_w_out.astype(BF16)
    sg_w_in_b = sg_w_in.astype(BF16)
    sg_w_out_b = sg_w_out.astype(BF16)
    sg_w_s_b = sg_w_s.astype(BF16)

    for layer in range(depth):
        i = layer // 2
        if layer % 2 == 0:
            shift, scale, gate = _split_mod(mod_ab[i], batch)
            proj = _in_projection(x, ab_norm_g[i], shift, scale, ab_w_in_b, i, cos, sin,
                                  epilogues=AB_EPILOGUES)
            ya, yb = _attn_conv(proj, ab_conv_w[i])
            x = _ab_out_projection(ya, yb, ab_w_out_b, i, x, gate)
        else:
            shift, scale, gate = _split_mod(mod_sg[i], batch)
            proj = _in_projection(x, sg_norm_g[i], shift, scale, sg_w_in_b, i, cos, sin,
                                  epilogues=SG_EPILOGUES)
            x = _sgu(proj, sg_ln_g[i], sg_ln_b[i], sg_w_s_b, sg_b_s[i].T, sg_w_out_b, i, x, gate,
                     final_norm_g, final=(layer == depth - 1))
    return x
```

```python
import functools

import jax
import jax.numpy as jnp
from jax import lax
from jax.experimental import pallas as pl
from jax.experimental.pallas import tpu as pltpu

F32 = jnp.float32
BF16 = jnp.bfloat16

D_MODEL = 2048
HEAD_DIM = 128
A_WIDTH = 1024
A_HEADS = 8
B_WIDTH = 1024
DILATIONS = (1, 4, 16)
RADIUS = 64
ROPE_THETA = 10000.0
NEG_INF = -1e30
C_WIDTH = 2048
C_GROUPS = 8
C_CHUNK = 128
EPS = 1e-6

Q_TILE = 128
K_WIN = Q_TILE + 2 * RADIUS
TILE_GROUP = 16
VMEM_LIMIT = 60 * 1024 * 1024

AB_EPILOGUES = (("rope_q", "rope_k"), ("none", "silu"), ("none", "none"), ("none", "silu"))
SG_EPILOGUES = (("gelu", "gelu"), ("gelu", "gelu"), ("silu", "silu"))


def _silu(z):
    hz = 0.5 * z
    return hz + hz * jnp.tanh(hz)


def _gelu_tanh(x):
    c = 0.7978845608028654
    return x * (0.5 * (1.0 + jnp.tanh(c * (x + 0.044715 * (x * x * x)))))


def _mod_kernel(c_ref, w_ref, b_ref, o_ref):
    a = _silu(c_ref[...]).astype(BF16)
    w = w_ref[...].astype(BF16)
    o_ref[...] = jnp.dot(a, w, preferred_element_type=F32) + b_ref[...]


def _modulation(c_pad, w_mod, b_mod, tn=1024):
    n_layers, d, n = w_mod.shape
    rows = c_pad.shape[0]
    return pl.pallas_call(
        _mod_kernel,
        grid=(n_layers, n // tn),
        in_specs=[
            pl.BlockSpec((rows, d), lambda l, j: (0, 0)),
            pl.BlockSpec((None, d, tn), lambda l, j: (l, 0, j)),
            pl.BlockSpec((None, 1, tn), lambda l, j: (l, 0, j)),
        ],
        out_specs=pl.BlockSpec((None, rows, tn), lambda l, j: (l, 0, j)),
        out_shape=jax.ShapeDtypeStruct((n_layers, rows, n), F32),
        compiler_params=pltpu.CompilerParams(
            dimension_semantics=("arbitrary", "arbitrary"),
            vmem_limit_bytes=VMEM_LIMIT),
        name="modulation",
    )(c_pad, w_mod, b_mod.reshape(n_layers, 1, n))


def _inproj_kernel(x_ref, g_ref, shift_ref, scale_ref, w_ref, cos_ref, sin_ref, o_ref,
                   h_scr, *, epilogues, seg, row_chunk, out_chunk):
    j = pl.program_id(2)
    tm = x_ref.shape[0]

    @pl.when(j == 0)
    def _():
        gm = g_ref[...] * (1.0 + scale_ref[...])
        shift = shift_ref[...]

        def body(c, carry):
            rows = pl.ds(pl.multiple_of(c * row_chunk, row_chunk), row_chunk)
            xf = x_ref[rows, :]
            ms = jnp.mean(xf * xf, axis=-1, keepdims=True)
            h_scr[rows, :] = (xf * lax.rsqrt(ms + EPS) * gm + shift).astype(BF16)
            return carry

        lax.fori_loop(0, tm // row_chunk, body, 0, unroll=4)

    def rope(acc, c0, f):
        cos = cos_ref[...] * f
        sin = sin_ref[...] * f
        for hh in range(seg // HEAD_DIM):
            cols = slice(c0 + hh * HEAD_DIM, c0 + (hh + 1) * HEAD_DIM)
            blk = acc[:, cols]
            o_ref[:, cols] = (blk * cos + pltpu.roll(blk, HEAD_DIM // 2, 1) * sin).astype(BF16)

    def pointwise(acc, c0, fn):
        for r0 in range(0, tm, out_chunk):
            o_ref[r0:r0 + out_chunk, c0:c0 + seg] = fn(
                acc[r0:r0 + out_chunk, c0:c0 + seg]).astype(BF16)

    finish = {
        "rope_q": lambda acc, c0: rope(acc, c0, HEAD_DIM ** -0.5),
        "rope_k": lambda acc, c0: rope(acc, c0, 1.0),
        "none": lambda acc, c0: pointwise(acc, c0, lambda t: t),
        "silu": lambda acc, c0: pointwise(acc, c0, _silu),
        "gelu": lambda acc, c0: pointwise(acc, c0, _gelu_tanh),
    }
    for kinds in sorted(set(epilogues)):
        tiles = [t for t, e in enumerate(epilogues) if e == kinds]
        cond = functools.reduce(jnp.logical_or, [j == t for t in tiles])

        @pl.when(cond)
        def _(kinds=kinds):
            acc = jnp.dot(h_scr[...], w_ref[...], preferred_element_type=F32)
            for si, kind in enumerate(kinds):
                finish[kind](acc, si * seg)


def _in_projection(x, norm_g, shift, scale, w_bf16, layer, cos, sin, *, epilogues, tm=1024):
    b, s, d = x.shape
    n = w_bf16.shape[2]
    tn = n // len(epilogues)
    seg = tn // len(epilogues[0])
    kern = functools.partial(_inproj_kernel, epilogues=epilogues, seg=seg, row_chunk=16,
                             out_chunk=256)
    return pl.pallas_call(
        kern,
        grid=(b, s // tm, n // tn),
        in_specs=[
            pl.BlockSpec((None, tm, d), lambda bi, i, j: (bi, i, 0)),
            pl.BlockSpec((1, d), lambda bi, i, j: (0, 0)),
            pl.BlockSpec((None, 1, d), lambda bi, i, j: (bi, 0, 0)),
            pl.BlockSpec((None, 1, d), lambda bi, i, j: (bi, 0, 0)),
            pl.BlockSpec((None, d, tn), lambda bi, i, j: (layer, 0, j)),
            pl.BlockSpec((tm, HEAD_DIM), lambda bi, i, j: (i, 0)),
            pl.BlockSpec((tm, HEAD_DIM), lambda bi, i, j: (i, 0)),
        ],
        out_specs=pl.BlockSpec((None, tm, tn), lambda bi, i, j: (bi, i, j)),
        out_shape=jax.ShapeDtypeStruct((b, s, n), BF16),
        scratch_shapes=[pltpu.VMEM((tm, d), BF16)],
        compiler_params=pltpu.CompilerParams(
            dimension_semantics=("arbitrary", "arbitrary", "arbitrary"),
            vmem_limit_bytes=VMEM_LIMIT),
        name="in_projection_ab" if "rope_q" in epilogues[0] else "in_projection_sg",
    )(x, norm_g.reshape(1, d), shift, scale, w_bf16, cos, sin)


def _attn_conv_kernel(q_ref, k_ref, v_ref, za_ref, ub_ref, gb_ref, gc_ref, zb_ref, cw_ref,
                      ya_ref, yb_ref,
                      fa, fb, fc, q4, k4, v4, q16, k16, v16, v1, mask_scr, pbuf):
    s = q_ref.shape[0]
    d4, d16 = DILATIONS[1], DILATIONS[2]
    n4, n16 = s // d4, s // d16
    step = d16 // d4
    chunk = 256
    mchunk = 128

    qi = lax.broadcasted_iota(jnp.int32, (Q_TILE, K_WIN), 0)
    ki = lax.broadcasted_iota(jnp.int32, (Q_TILE, K_WIN), 1)
    for t in range(3):
        mask_scr[t] = jnp.where(jnp.abs(qi + t * RADIUS - ki) <= RADIUS, 0.0, NEG_INF)

    ones = jnp.ones((s, HEAD_DIM), BF16)
    v1[:, HEAD_DIM:2 * HEAD_DIM] = ones
    v4[:, HEAD_DIM:2 * HEAD_DIM] = ones
    v16[:, HEAD_DIM:2 * HEAD_DIM] = ones
    v1[:, 0:HEAD_DIM] = v_ref[...]

    for idx, (src, dst4, dst16) in enumerate(((q_ref, q4, q16), (k_ref, k4, k16),
                                              (v_ref, v4, v16))):
        for c0 in range(0, s, chunk):
            fa[idx, c0:c0 + chunk, :] = src[c0:c0 + chunk, :].astype(F32)
        for r in range(d4):
            for c0 in range(0, n4, chunk):
                part = fa[idx, pl.ds(r + d4 * c0, chunk, stride=d4), :]
                fb[idx, r * n4 + c0:r * n4 + c0 + chunk, :] = part
                dst4[r * n4 + c0:r * n4 + c0 + chunk, 0:HEAD_DIM] = part.astype(BF16)
        for r in range(d4):
            for a in range(step):
                part = fb[idx, pl.ds(r * n4 + a, n16, stride=step), :]
                r16 = r + d4 * a
                dst16[r16 * n16:(r16 + 1) * n16, 0:HEAD_DIM] = part.astype(BF16)

    for p, (dil, qs, ks, vs) in enumerate(((1, q_ref, k_ref, v1), (d4, q4, k4, v4),
                                           (d16, q16, k16, v16))):
        n = s // dil
        tiles_per_seg = n // Q_TILE

        def one_tile(t, qs=qs, ks=ks, vs=vs, n=n, tiles_per_seg=tiles_per_seg, p=p):
            seg = t // tiles_per_seg
            l0 = (t % tiles_per_seg) * Q_TILE
            kstart = jnp.clip(l0 - RADIUS, 0, n - K_WIN)
            which = (l0 - kstart) // RADIUS
            row0 = pl.multiple_of(seg * n + l0, Q_TILE)
            krow0 = pl.multiple_of(seg * n + kstart, RADIUS)
            qt = qs[pl.ds(row0, Q_TILE), :]
            kt = ks[pl.ds(krow0, K_WIN), :]
            vt = vs[pl.ds(krow0, K_WIN), :]
            sc = lax.dot_general(qt, kt, (((1,), (1,)), ((), ())),
                                 preferred_element_type=F32)
            sc = sc + mask_scr[which]
            m = jnp.max(sc, axis=-1, keepdims=True)
            e = jnp.exp(sc - m).astype(BF16)
            ov = jnp.dot(e, vt, preferred_element_type=F32)
            fa[p, pl.ds(row0, Q_TILE), :] = ov[:, 0:HEAD_DIM]
            fb[p, pl.ds(row0, Q_TILE), :] = jnp.broadcast_to(m, (Q_TILE, HEAD_DIM))
            fc[p, pl.ds(row0, Q_TILE), :] = ov[:, HEAD_DIM:2 * HEAD_DIM]

        def group_body(gi, carry, one_tile=one_tile):
            for u in range(TILE_GROUP):
                one_tile(gi * TILE_GROUP + u)
            return carry

        lax.fori_loop(0, s // (Q_TILE * TILE_GROUP), group_body, 0)

    for r in range(d4):
        for a in range(step):
            for c0 in range(0, n16, mchunk):
                rows4 = pl.ds(r * n4 + a + step * c0, mchunk, stride=step)
                rows16 = pl.ds((r + d4 * a) * n16 + c0, mchunk)
                m1 = fb[1, rows4, :]
                m2 = fb[2, rows16, :]
                m = jnp.maximum(m1, m2)
                e1 = jnp.exp(m1 - m)
                e2 = jnp.exp(m2 - m)
                fa[1, rows4, :] = e1 * fa[1, rows4, :] + e2 * fa[2, rows16, :]
                fc[1, rows4, :] = e1 * fc[1, rows4, :] + e2 * fc[2, rows16, :]
                fb[1, rows4, :] = m
    for r in range(d4):
        for c0 in range(0, n4, mchunk):
            nat = pl.ds(r + d4 * c0, mchunk, stride=d4)
            rows = pl.ds(r * n4 + c0, mchunk)
            m0 = fb[0, nat, :]
            m12 = fb[1, rows, :]
            m = jnp.maximum(m0, m12)
            e0 = jnp.exp(m0 - m)
            e12 = jnp.exp(m12 - m)
            num = e0 * fa[0, nat, :] + e12 * fa[1, rows, :]
            den = e0 * fc[0, nat, :] + e12 * fc[1, rows, :]
            fa[0, nat, :] = num / den

    for c0 in range(0, s, chunk):
        rows = slice(c0, c0 + chunk)
        ya_ref[rows, :] = (fa[0, rows, :] * za_ref[rows, :].astype(F32)).astype(BF16)

    zeros8 = jnp.zeros((8, HEAD_DIM), F32)
    pbuf[0:8, :] = zeros8
    pbuf[8 + s:16 + s, :] = zeros8
    for c0 in range(0, s, chunk):
        rows = slice(c0, c0 + chunk)
        pbuf[8 + c0:8 + c0 + chunk, :] = gc_ref[rows, :].astype(F32) * ub_ref[rows, :].astype(F32)
    w0 = cw_ref[0:1, :]
    w1 = cw_ref[1:2, :]
    w2 = cw_ref[2:3, :]
    for c0 in range(0, s, chunk):
        conv = (w0 * pbuf[7 + c0:7 + c0 + chunk, :] + w1 * pbuf[8 + c0:8 + c0 + chunk, :]
                + w2 * pbuf[9 + c0:9 + c0 + chunk, :])
        gate = gb_ref[c0:c0 + chunk, :].astype(F32) * zb_ref[c0:c0 + chunk, :].astype(F32)
        yb_ref[c0:c0 + chunk, :] = (gate * conv).astype(BF16)


def _attn_conv(proj, conv_w):
    b, s, _ = proj.shape
    nblk = A_WIDTH // HEAD_DIM

    def col(slot):
        return pl.BlockSpec((None, s, HEAD_DIM), lambda bi, h: (bi, 0, slot * nblk + h))

    out_spec = pl.BlockSpec((None, s, HEAD_DIM), lambda bi, h: (bi, 0, h))
    return pl.pallas_call(
        _attn_conv_kernel,
        grid=(b, A_HEADS),
        in_specs=[col(0), col(1), col(2), col(3), col(4), col(5), col(6), col(7),
                  pl.BlockSpec((3, HEAD_DIM), lambda bi, h: (0, h))],
        out_specs=[out_spec, out_spec],
        out_shape=[jax.ShapeDtypeStruct((b, s, A_WIDTH), BF16),
                   jax.ShapeDtypeStruct((b, s, B_WIDTH), BF16)],
        scratch_shapes=[
            pltpu.VMEM((3, s, HEAD_DIM), F32), pltpu.VMEM((3, s, HEAD_DIM), F32),
            pltpu.VMEM((3, s, HEAD_DIM), F32),
            pltpu.VMEM((s, HEAD_DIM), BF16), pltpu.VMEM((s, HEAD_DIM), BF16),
            pltpu.VMEM((s, 2 * HEAD_DIM), BF16),
            pltpu.VMEM((s, HEAD_DIM), BF16), pltpu.VMEM((s, HEAD_DIM), BF16),
            pltpu.VMEM((s, 2 * HEAD_DIM), BF16),
            pltpu.VMEM((s, 2 * HEAD_DIM), BF16),
            pltpu.VMEM((3, Q_TILE, K_WIN), F32),
            pltpu.VMEM((s + 16, HEAD_DIM), F32),
        ],
        compiler_params=pltpu.CompilerParams(
            dimension_semantics=("arbitrary", "arbitrary"),
            vmem_limit_bytes=VMEM_LIMIT),
        name="attn_conv",
    )(proj, proj, proj, proj, proj, proj, proj, proj, conv_w)


def _ab_out_kernel(ya_ref, yb_ref, w_ref, x_ref, gate_ref, o_ref):
    out = jnp.dot(ya_ref[...], w_ref[0:A_WIDTH, :], preferred_element_type=F32)
    out = out + jnp.dot(yb_ref[...], w_ref[A_WIDTH:A_WIDTH + B_WIDTH, :],
                        preferred_element_type=F32)
    o_ref[...] = x_ref[...] + gate_ref[...] * out


def _ab_out_projection(ya, yb, w_bf16, layer, x, gate, tm=512):
    b, s, d = x.shape
    return pl.pallas_call(
        _ab_out_kernel,
        grid=(b, s // tm),
        in_specs=[
            pl.BlockSpec((None, tm, A_WIDTH), lambda bi, i: (bi, i, 0)),
            pl.BlockSpec((None, tm, B_WIDTH), lambda bi, i: (bi, i, 0)),
            pl.BlockSpec((None, A_WIDTH + B_WIDTH, d), lambda bi, i: (layer, 0, 0)),
            pl.BlockSpec((None, tm, d), lambda bi, i: (bi, i, 0)),
            pl.BlockSpec((None, 1, d), lambda bi, i: (bi, 0, 0)),
        ],
        out_specs=pl.BlockSpec((None, tm, d), lambda bi, i: (bi, i, 0)),
        out_shape=jax.ShapeDtypeStruct((b, s, d), F32),
        compiler_params=pltpu.CompilerParams(
            dimension_semantics=("arbitrary", "arbitrary"),
            vmem_limit_bytes=VMEM_LIMIT),
        name="ab_out_projection",
    )(ya, yb, w_bf16, x, gate)


def _sgu_kernel(u_ref, v_ref, z_ref, lng_ref, lnb_ref, ws_ref, bs_ref, w_ref, x_ref, gate_ref,
                fg_ref, o_ref, y_scr, *, final, dot_rows):
    tm = u_ref.shape[0]
    gw = C_WIDTH // C_GROUPS
    lng = lng_ref[...]
    lnb = lnb_ref[...]
    for d0 in range(0, tm, dot_rows):
        for c0 in range(d0, d0 + dot_rows, C_CHUNK):
            rows = slice(c0, c0 + C_CHUNK)
            v = v_ref[rows, :].astype(F32)
            mu = jnp.mean(v, axis=-1, keepdims=True)
            vc = v - mu
            var = jnp.mean(vc * vc, axis=-1, keepdims=True)
            vn = (vc * lax.rsqrt(var + EPS) * lng + lnb).astype(BF16)
            for g in range(C_GROUPS):
                cols = slice(g * gw, (g + 1) * gw)
                mixed = jnp.dot(ws_ref[g], vn[:, cols], preferred_element_type=F32)
                mixed = mixed + bs_ref[:, g:g + 1]
                gated = u_ref[rows, cols].astype(F32) * z_ref[rows, cols].astype(F32)
                y_scr[rows, cols] = (gated * mixed).astype(BF16)
        rows = slice(d0, d0 + dot_rows)
        out = jnp.dot(y_scr[rows, :], w_ref[...], preferred_element_type=F32)
        xn = x_ref[rows, :] + gate_ref[...] * out
        if final:
            ms = jnp.mean(xn * xn, axis=-1, keepdims=True)
            xn = xn * lax.rsqrt(ms + EPS) * fg_ref[...]
        o_ref[rows, :] = xn


def _sgu(proj, ln_g, ln_b, ws_bf16, bs_t, w_bf16, layer, x, gate, final_g, *, final, tm=512):
    b, s, d = x.shape
    cw = C_WIDTH
    kern = functools.partial(_sgu_kernel, final=final, dot_rows=256)
    return pl.pallas_call(
        kern,
        grid=(b, s // tm),
        in_specs=[
            pl.BlockSpec((None, tm, cw), lambda bi, i: (bi, i, 0)),
            pl.BlockSpec((None, tm, cw), lambda bi, i: (bi, i, 1)),
            pl.BlockSpec((None, tm, cw), lambda bi, i: (bi, i, 2)),
            pl.BlockSpec((1, cw), lambda bi, i: (0, 0)),
            pl.BlockSpec((1, cw), lambda bi, i: (0, 0)),
            pl.BlockSpec((None, C_GROUPS, C_CHUNK, C_CHUNK), lambda bi, i: (layer, 0, 0, 0)),
            pl.BlockSpec((C_CHUNK, C_GROUPS), lambda bi, i: (0, 0)),
            pl.BlockSpec((None, cw, d), lambda bi, i: (layer, 0, 0), pipeline_mode=pl.Buffered(1)),
            pl.BlockSpec((None, tm, d), lambda bi, i: (bi, i, 0)),
            pl.BlockSpec((None, 1, d), lambda bi, i: (bi, 0, 0)),
            pl.BlockSpec((1, d), lambda bi, i: (0, 0)),
        ],
        out_specs=pl.BlockSpec((None, tm, d), lambda bi, i: (bi, i, 0)),
        out_shape=jax.ShapeDtypeStruct((b, s, d), F32),
        scratch_shapes=[pltpu.VMEM((tm, cw), BF16)],
        compiler_params=pltpu.CompilerParams(
            dimension_semantics=("arbitrary", "arbitrary"),
            vmem_limit_bytes=VMEM_LIMIT),
        name="sgu_final" if final else "sgu",
    )(proj, proj, proj, ln_g.reshape(1, cw), ln_b.reshape(1, cw), ws_bf16, bs_t, w_bf16, x,
      gate, final_g.reshape(1, d))


def _rope_tables(s):
    half = HEAD_DIM // 2
    inv = ROPE_THETA ** (-jnp.arange(half, dtype=F32) / half)
    ang = jnp.arange(s, dtype=F32)[:, None] * inv[None, :]
    cos = jnp.cos(ang)
    sin = jnp.sin(ang)
    return (jnp.concatenate([cos, cos], axis=-1), jnp.concatenate([-sin, sin], axis=-1))


def _split_mod(mod, batch):
    m = mod[:batch].reshape(batch, 1, 3, D_MODEL)
    return m[:, :, 0, :], m[:, :, 1, :], m[:, :, 2, :]


def kernel(x, c, ab_norm_g, ab_w_mod, ab_b_mod, ab_w_in, ab_conv_w, ab_w_out, sg_norm_g, sg_w_mod, sg_b_mod, sg_w_in, sg_ln_g, sg_ln_b, sg_w_s, sg_b_s, sg_w_out, final_norm_g):
    batch, s, _ = x.shape
    depth = ab_w_in.shape[0] + sg_w_in.shape[0]
    assert depth % 2 == 0
    c_pad = jnp.pad(c, ((0, 16 - batch), (0, 0)))
    mod_ab = _modulation(c_pad, ab_w_mod, ab_b_mod)
    mod_sg = _modulation(c_pad, sg_w_mod, sg_b_mod)
    cos, sin = _rope_tables(s)

    ab_w_in_b = ab_w_in.astype(BF16)
    ab_w_out_b = ab_w_out.astype(BF16)
    sg_w_in_b = sg_w_in.astype(BF16)
    sg_w_out_b = sg_w_out.astype(BF16)
    sg_w_s_b = sg_w_s.astype(BF16)

    for layer in range(depth):
        i = layer // 2
        if layer % 2 == 0:
            shift, scale, gate = _split_mod(mod_ab[i], batch)
            proj = _in_projection(x, ab_norm_g[i], shift, scale, ab_w_in_b, i, cos, sin,
                                  epilogues=AB_EPILOGUES)
            ya, yb = _attn_conv(proj, ab_conv_w[i])
            x = _ab_out_projection(ya, yb, ab_w_out_b, i, x, gate)
        else:
            shift, scale, gate = _split_mod(mod_sg[i], batch)
            proj = _in_projection(x, sg_norm_g[i], shift, scale, sg_w_in_b, i, cos, sin,
                                  epilogues=SG_EPILOGUES)
            x = _sgu(proj, sg_ln_g[i], sg_ln_b[i], sg_w_s_b, sg_b_s[i].T, sg_w_out_b, i, x, gate,
                     final_norm_g, final=(layer == depth - 1))
    return x
```

```python
import functools

import jax
import jax.numpy as jnp
from jax import lax
from jax.experimental import pallas as pl
from jax.experimental.pallas import tpu as pltpu

F32 = jnp.float32
BF16 = jnp.bfloat16

D_MODEL = 2048
HEAD_DIM = 128
A_WIDTH = 1024
A_HEADS = 8
B_WIDTH = 1024
DILATIONS = (1, 4, 16)
RADIUS = 64
ROPE_THETA = 10000.0
NEG_INF = -1e30
C_WIDTH = 2048
C_GROUPS = 8
C_CHUNK = 128
EPS = 1e-6

Q_TILE = 128
K_WIN = Q_TILE + 2 * RADIUS
TILE_GROUP = 16
VMEM_LIMIT = 60 * 1024 * 1024

AB_EPILOGUES = (("rope_q", "rope_k"), ("none", "silu"), ("none", "none"), ("none", "silu"))
SG_EPILOGUES = (("gelu", "gelu"), ("gelu", "gelu"), ("silu", "silu"))


def _silu(z):
    hz = 0.5 * z
    return hz + hz * jnp.tanh(hz)


def _gelu_tanh(x):
    c = 0.7978845608028654
    return x * (0.5 * (1.0 + jnp.tanh(c * (x + 0.044715 * (x * x * x)))))


def _rms_norm(x, gain):
    ms = jnp.mean(x * x, axis=-1, keepdims=True)
    return x * lax.rsqrt(ms + EPS) * gain


def _modulated_rms_norm(x, gm, shift):
    return _rms_norm(x, gm) + shift


def _residual_tail(xn, rows, ng_ref, nshift_ref, nscale_ref, o_ref, h_ref, final):
    if final:
        o_ref[rows, :] = _rms_norm(xn, ng_ref[...])
    else:
        o_ref[rows, :] = xn
        gm = ng_ref[...] * (1.0 + nscale_ref[...])
        h_ref[rows, :] = _modulated_rms_norm(xn, gm, nshift_ref[...]).astype(BF16)


def _mod_kernel(c_ref, w_ref, b_ref, o_ref):
    a = _silu(c_ref[...]).astype(BF16)
    w = w_ref[...].astype(BF16)
    o_ref[...] = jnp.dot(a, w, preferred_element_type=F32) + b_ref[...]


def _modulation(c_pad, w_mod, b_mod, tn=1024):
    n_layers, d, n = w_mod.shape
    rows = c_pad.shape[0]
    return pl.pallas_call(
        _mod_kernel,
        grid=(n_layers, n // tn),
        in_specs=[
            pl.BlockSpec((rows, d), lambda l, j: (0, 0)),
            pl.BlockSpec((None, d, tn), lambda l, j: (l, 0, j)),
            pl.BlockSpec((None, 1, tn), lambda l, j: (l, 0, j)),
        ],
        out_specs=pl.BlockSpec((None, rows, tn), lambda l, j: (l, 0, j)),
        out_shape=jax.ShapeDtypeStruct((n_layers, rows, n), F32),
        compiler_params=pltpu.CompilerParams(
            dimension_semantics=("arbitrary", "arbitrary"),
            vmem_limit_bytes=VMEM_LIMIT),
        name="modulation",
    )(c_pad, w_mod, b_mod.reshape(n_layers, 1, n))


def _inproj_kernel(x_ref, g_ref, shift_ref, scale_ref, w_ref, cos_ref, sin_ref, o_ref,
                   *scratch, epilogues, seg, normalize, row_chunk, out_chunk):
    j = pl.program_id(2)
    tm = x_ref.shape[0]

    if normalize:
        h_ref, = scratch

        @pl.when(j == 0)
        def _():
            gm = g_ref[...] * (1.0 + scale_ref[...])
            shift = shift_ref[...]

            def body(c, carry):
                rows = pl.ds(pl.multiple_of(c * row_chunk, row_chunk), row_chunk)
                h_ref[rows, :] = _modulated_rms_norm(x_ref[rows, :], gm, shift).astype(BF16)
                return carry

            lax.fori_loop(0, tm // row_chunk, body, 0, unroll=4)
    else:
        h_ref = x_ref

    def rope(acc, c0, f):
        cos = cos_ref[...] * f
        sin = sin_ref[...] * f
        for hh in range(seg // HEAD_DIM):
            cols = slice(c0 + hh * HEAD_DIM, c0 + (hh + 1) * HEAD_DIM)
            blk = acc[:, cols]
            o_ref[:, cols] = (blk * cos + pltpu.roll(blk, HEAD_DIM // 2, 1) * sin).astype(BF16)

    def pointwise(acc, c0, fn):
        for r0 in range(0, tm, out_chunk):
            o_ref[r0:r0 + out_chunk, c0:c0 + seg] = fn(
                acc[r0:r0 + out_chunk, c0:c0 + seg]).astype(BF16)

    finish = {
        "rope_q": lambda acc, c0: rope(acc, c0, HEAD_DIM ** -0.5),
        "rope_k": lambda acc, c0: rope(acc, c0, 1.0),
        "none": lambda acc, c0: pointwise(acc, c0, lambda t: t),
        "silu": lambda acc, c0: pointwise(acc, c0, _silu),
        "gelu": lambda acc, c0: pointwise(acc, c0, _gelu_tanh),
    }
    for kinds in sorted(set(epilogues)):
        tiles = [t for t, e in enumerate(epilogues) if e == kinds]
        cond = functools.reduce(jnp.logical_or, [j == t for t in tiles])

        @pl.when(cond)
        def _(kinds=kinds):
            acc = jnp.dot(h_ref[...], w_ref[...], preferred_element_type=F32)
            for si, kind in enumerate(kinds):
                finish[kind](acc, si * seg)


def _in_projection(x, norm_g, shift, scale, w_bf16, layer, cos, sin, *, epilogues, normalize,
                   tm=1024):
    b, s, d = x.shape
    n = w_bf16.shape[2]
    tn = n // len(epilogues)
    seg = tn // len(epilogues[0])
    kern = functools.partial(_inproj_kernel, epilogues=epilogues, seg=seg, normalize=normalize,
                             row_chunk=16, out_chunk=256)
    return pl.pallas_call(
        kern,
        grid=(b, s // tm, n // tn),
        in_specs=[
            pl.BlockSpec((None, tm, d), lambda bi, i, j: (bi, i, 0)),
            pl.BlockSpec((1, d), lambda bi, i, j: (0, 0)),
            pl.BlockSpec((None, 1, d), lambda bi, i, j: (bi, 0, 0)),
            pl.BlockSpec((None, 1, d), lambda bi, i, j: (bi, 0, 0)),
            pl.BlockSpec((None, d, tn), lambda bi, i, j: (layer, 0, j)),
            pl.BlockSpec((tm, HEAD_DIM), lambda bi, i, j: (i, 0)),
            pl.BlockSpec((tm, HEAD_DIM), lambda bi, i, j: (i, 0)),
        ],
        out_specs=pl.BlockSpec((None, tm, tn), lambda bi, i, j: (bi, i, j)),
        out_shape=jax.ShapeDtypeStruct((b, s, n), BF16),
        scratch_shapes=[pltpu.VMEM((tm, d), BF16)] if normalize else [],
        compiler_params=pltpu.CompilerParams(
            dimension_semantics=("arbitrary", "arbitrary", "arbitrary"),
            vmem_limit_bytes=VMEM_LIMIT),
        name="in_projection_ab" if "rope_q" in epilogues[0] else "in_projection_sg",
    )(x, norm_g.reshape(1, d), shift, scale, w_bf16, cos, sin)


def _attn_conv_kernel(q_ref, k_ref, v_ref, za_ref, ub_ref, gb_ref, gc_ref, zb_ref, cw_ref,
                      ya_ref, yb_ref,
                      fa, fb, fc, q4, k4, v4, q16, k16, v16, v1, mask_scr, pbuf):
    s = q_ref.shape[0]
    d4, d16 = DILATIONS[1], DILATIONS[2]
    n4, n16 = s // d4, s // d16
    step = d16 // d4
    chunk = 256
    mchunk = 128

    qi = lax.broadcasted_iota(jnp.int32, (Q_TILE, K_WIN), 0)
    ki = lax.broadcasted_iota(jnp.int32, (Q_TILE, K_WIN), 1)
    for t in range(3):
        mask_scr[t] = jnp.where(jnp.abs(qi + t * RADIUS - ki) <= RADIUS, 0.0, NEG_INF)

    ones = jnp.ones((s, HEAD_DIM), BF16)
    v1[:, HEAD_DIM:2 * HEAD_DIM] = ones
    v4[:, HEAD_DIM:2 * HEAD_DIM] = ones
    v16[:, HEAD_DIM:2 * HEAD_DIM] = ones
    v1[:, 0:HEAD_DIM] = v_ref[...]

    for idx, (src, dst4, dst16) in enumerate(((q_ref, q4, q16), (k_ref, k4, k16),
                                              (v_ref, v4, v16))):
        for c0 in range(0, s, chunk):
            fa[idx, c0:c0 + chunk, :] = src[c0:c0 + chunk, :].astype(F32)
        for r in range(d4):
            for c0 in range(0, n4, chunk):
                part = fa[idx, pl.ds(r + d4 * c0, chunk, stride=d4), :]
                fb[idx, r * n4 + c0:r * n4 + c0 + chunk, :] = part
                dst4[r * n4 + c0:r * n4 + c0 + chunk, 0:HEAD_DIM] = part.astype(BF16)
        for r in range(d4):
            for a in range(step):
                part = fb[idx, pl.ds(r * n4 + a, n16, stride=step), :]
                r16 = r + d4 * a
                dst16[r16 * n16:(r16 + 1) * n16, 0:HEAD_DIM] = part.astype(BF16)

    for p, (dil, qs, ks, vs) in enumerate(((1, q_ref, k_ref, v1), (d4, q4, k4, v4),
                                           (d16, q16, k16, v16))):
        n = s // dil
        tiles_per_seg = n // Q_TILE

        def one_tile(t, qs=qs, ks=ks, vs=vs, n=n, tiles_per_seg=tiles_per_seg, p=p):
            seg = t // tiles_per_seg
            l0 = (t % tiles_per_seg) * Q_TILE
            kstart = jnp.clip(l0 - RADIUS, 0, n - K_WIN)
            which = (l0 - kstart) // RADIUS
            row0 = pl.multiple_of(seg * n + l0, Q_TILE)
            krow0 = pl.multiple_of(seg * n + kstart, RADIUS)
            qt = qs[pl.ds(row0, Q_TILE), :]
            kt = ks[pl.ds(krow0, K_WIN), :]
            vt = vs[pl.ds(krow0, K_WIN), :]
            sc = lax.dot_general(qt, kt, (((1,), (1,)), ((), ())),
                                 preferred_element_type=F32)
            sc = sc + mask_scr[which]
            m = jnp.max(sc, axis=-1, keepdims=True)
            e = jnp.exp(sc - m).astype(BF16)
            ov = jnp.dot(e, vt, preferred_element_type=F32)
            fa[p, pl.ds(row0, Q_TILE), :] = ov[:, 0:HEAD_DIM]
            fb[p, pl.ds(row0, Q_TILE), :] = jnp.broadcast_to(m, (Q_TILE, HEAD_DIM))
            fc[p, pl.ds(row0, Q_TILE), :] = ov[:, HEAD_DIM:2 * HEAD_DIM]

        def group_body(gi, carry, one_tile=one_tile):
            for u in range(TILE_GROUP):
                one_tile(gi * TILE_GROUP + u)
            return carry

        lax.fori_loop(0, s // (Q_TILE * TILE_GROUP), group_body, 0)

    for r in range(d4):
        for a in range(step):
            for c0 in range(0, n16, mchunk):
                rows4 = pl.ds(r * n4 + a + step * c0, mchunk, stride=step)
                rows16 = pl.ds((r + d4 * a) * n16 + c0, mchunk)
                m1 = fb[1, rows4, :]
                m2 = fb[2, rows16, :]
                m = jnp.maximum(m1, m2)
                e1 = jnp.exp(m1 - m)
                e2 = jnp.exp(m2 - m)
                fa[1, rows4, :] = e1 * fa[1, rows4, :] + e2 * fa[2, rows16, :]
                fc[1, rows4, :] = e1 * fc[1, rows4, :] + e2 * fc[2, rows16, :]
                fb[1, rows4, :] = m
    for r in range(d4):
        for c0 in range(0, n4, mchunk):
            nat = pl.ds(r + d4 * c0, mchunk, stride=d4)
            rows = pl.ds(r * n4 + c0, mchunk)
            m0 = fb[0, nat, :]
            m12 = fb[1, rows, :]
            m = jnp.maximum(m0, m12)
            e0 = jnp.exp(m0 - m)
            e12 = jnp.exp(m12 - m)
            num = e0 * fa[0, nat, :] + e12 * fa[1, rows, :]
            den = e0 * fc[0, nat, :] + e12 * fc[1, rows, :]
            fa[0, nat, :] = num / den

    for c0 in range(0, s, chunk):
        rows = slice(c0, c0 + chunk)
        ya_ref[rows, :] = (fa[0, rows, :] * za_ref[rows, :].astype(F32)).astype(BF16)

    zeros8 = jnp.zeros((8, HEAD_DIM), F32)
    pbuf[0:8, :] = zeros8
    pbuf[8 + s:16 + s, :] = zeros8
    for c0 in range(0, s, chunk):
        rows = slice(c0, c0 + chunk)
        pbuf[8 + c0:8 + c0 + chunk, :] = gc_ref[rows, :].astype(F32) * ub_ref[rows, :].astype(F32)
    w0 = cw_ref[0:1, :]
    w1 = cw_ref[1:2, :]
    w2 = cw_ref[2:3, :]
    for c0 in range(0, s, chunk):
        conv = (w0 * pbuf[7 + c0:7 + c0 + chunk, :] + w1 * pbuf[8 + c0:8 + c0 + chunk, :]
                + w2 * pbuf[9 + c0:9 + c0 + chunk, :])
        gate = gb_ref[c0:c0 + chunk, :].astype(F32) * zb_ref[c0:c0 + chunk, :].astype(F32)
        yb_ref[c0:c0 + chunk, :] = (gate * conv).astype(BF16)


def _attn_conv(proj, conv_w):
    b, s, _ = proj.shape
    nblk = A_WIDTH // HEAD_DIM

    def col(slot):
        return pl.BlockSpec((None, s, HEAD_DIM), lambda bi, h: (bi, 0, slot * nblk + h))

    out_spec = pl.BlockSpec((None, s, HEAD_DIM), lambda bi, h: (bi, 0, h))
    return pl.pallas_call(
        _attn_conv_kernel,
        grid=(b, A_HEADS),
        in_specs=[col(0), col(1), col(2), col(3), col(4), col(5), col(6), col(7),
                  pl.BlockSpec((3, HEAD_DIM), lambda bi, h: (0, h))],
        out_specs=[out_spec, out_spec],
        out_shape=[jax.ShapeDtypeStruct((b, s, A_WIDTH), BF16),
                   jax.ShapeDtypeStruct((b, s, B_WIDTH), BF16)],
        scratch_shapes=[
            pltpu.VMEM((3, s, HEAD_DIM), F32), pltpu.VMEM((3, s, HEAD_DIM), F32),
            pltpu.VMEM((3, s, HEAD_DIM), F32),
            pltpu.VMEM((s, HEAD_DIM), BF16), pltpu.VMEM((s, HEAD_DIM), BF16),
            pltpu.VMEM((s, 2 * HEAD_DIM), BF16),
            pltpu.VMEM((s, HEAD_DIM), BF16), pltpu.VMEM((s, HEAD_DIM), BF16),
            pltpu.VMEM((s, 2 * HEAD_DIM), BF16),
            pltpu.VMEM((s, 2 * HEAD_DIM), BF16),
            pltpu.VMEM((3, Q_TILE, K_WIN), F32),
            pltpu.VMEM((s + 16, HEAD_DIM), F32),
        ],
        compiler_params=pltpu.CompilerParams(
            dimension_semantics=("arbitrary", "arbitrary"),
            vmem_limit_bytes=VMEM_LIMIT),
        name="attn_conv",
    )(proj, proj, proj, proj, proj, proj, proj, proj, conv_w)


def _ab_out_kernel(ya_ref, yb_ref, w_ref, x_ref, gate_ref, ng_ref, nshift_ref, nscale_ref,
                   o_ref, *h_out, final, dot_rows):
    h_ref = None if final else h_out[0]
    for d0 in range(0, x_ref.shape[0], dot_rows):
        rows = slice(d0, d0 + dot_rows)
        out = jnp.dot(ya_ref[rows, :], w_ref[0:A_WIDTH, :], preferred_element_type=F32)
        out = out + jnp.dot(yb_ref[rows, :], w_ref[A_WIDTH:A_WIDTH + B_WIDTH, :],
                            preferred_element_type=F32)
        xn = x_ref[rows, :] + gate_ref[...] * out
        _residual_tail(xn, rows, ng_ref, nshift_ref, nscale_ref, o_ref, h_ref, final)


def _residual_out(b, s, d, tm, final):
    spec = pl.BlockSpec((None, tm, d), lambda bi, i: (bi, i, 0))
    if final:
        return spec, jax.ShapeDtypeStruct((b, s, d), F32)
    return [spec, spec], [jax.ShapeDtypeStruct((b, s, d), F32),
                          jax.ShapeDtypeStruct((b, s, d), BF16)]


def _ab_out_projection(ya, yb, w_bf16, layer, x, gate, next_g, next_shift, next_scale, *, final,
                       tm=512):
    b, s, d = x.shape
    out_specs, out_shape = _residual_out(b, s, d, tm, final)
    row_vec = pl.BlockSpec((None, 1, d), lambda bi, i: (bi, 0, 0))
    return pl.pallas_call(
        functools.partial(_ab_out_kernel, final=final, dot_rows=256),
        grid=(b, s // tm),
        in_specs=[
            pl.BlockSpec((None, tm, A_WIDTH), lambda bi, i: (bi, i, 0)),
            pl.BlockSpec((None, tm, B_WIDTH), lambda bi, i: (bi, i, 0)),
            pl.BlockSpec((None, A_WIDTH + B_WIDTH, d), lambda bi, i: (layer, 0, 0),
                         pipeline_mode=pl.Buffered(1)),
            pl.BlockSpec((None, tm, d), lambda bi, i: (bi, i, 0)),
            row_vec,
            pl.BlockSpec((1, d), lambda bi, i: (0, 0)),
            row_vec,
            row_vec,
        ],
        out_specs=out_specs,
        out_shape=out_shape,
        compiler_params=pltpu.CompilerParams(
            dimension_semantics=("arbitrary", "arbitrary"),
            vmem_limit_bytes=VMEM_LIMIT),
        name="ab_out_projection",
    )(ya, yb, w_bf16, x, gate, next_g.reshape(1, d), next_shift, next_scale)


def _sgu_kernel(u_ref, v_ref, z_ref, lng_ref, lnb_ref, ws_ref, bs_ref, w_ref, x_ref, gate_ref,
                ng_ref, nshift_ref, nscale_ref, o_ref, *rest, final, dot_rows):
    h_ref, y_scr = (None, rest[0]) if final else rest
    tm = u_ref.shape[0]
    gw = C_WIDTH // C_GROUPS
    lng = lng_ref[...]
    lnb = lnb_ref[...]
    for d0 in range(0, tm, dot_rows):
        for c0 in range(d0, d0 + dot_rows, C_CHUNK):
            rows = slice(c0, c0 + C_CHUNK)
            v = v_ref[rows, :].astype(F32)
            mu = jnp.mean(v, axis=-1, keepdims=True)
            vc = v - mu
            var = jnp.mean(vc * vc, axis=-1, keepdims=True)
            vn = (vc * lax.rsqrt(var + EPS) * lng + lnb).astype(BF16)
            for g in range(C_GROUPS):
                cols = slice(g * gw, (g + 1) * gw)
                mixed = jnp.dot(ws_ref[g], vn[:, cols], preferred_element_type=F32)
                mixed = mixed + bs_ref[:, g:g + 1]
                gated = u_ref[rows, cols].astype(F32) * z_ref[rows, cols].astype(F32)
                y_scr[rows, cols] = (gated * mixed).astype(BF16)
        rows = slice(d0, d0 + dot_rows)
        out = jnp.dot(y_scr[rows, :], w_ref[...], preferred_element_type=F32)
        xn = x_ref[rows, :] + gate_ref[...] * out
        _residual_tail(xn, rows, ng_ref, nshift_ref, nscale_ref, o_ref, h_ref, final)


def _sgu(proj, ln_g, ln_b, ws_bf16, bs_t, w_bf16, layer, x, gate, next_g, next_shift, next_scale, *,
         final, tm=512):
    b, s, d = x.shape
    cw = C_WIDTH
    kern = functools.partial(_sgu_kernel, final=final, dot_rows=256)
    out_specs, out_shape = _residual_out(b, s, d, tm, final)
    row_vec = pl.BlockSpec((None, 1, d), lambda bi, i: (bi, 0, 0))
    return pl.pallas_call(
        kern,
        grid=(b, s // tm),
        in_specs=[
            pl.BlockSpec((None, tm, cw), lambda bi, i: (bi, i, 0)),
            pl.BlockSpec((None, tm, cw), lambda bi, i: (bi, i, 1)),
            pl.BlockSpec((None, tm, cw), lambda bi, i: (bi, i, 2)),
            pl.BlockSpec((1, cw), lambda bi, i: (0, 0)),
            pl.BlockSpec((1, cw), lambda bi, i: (0, 0)),
            pl.BlockSpec((None, C_GROUPS, C_CHUNK, C_CHUNK), lambda bi, i: (layer, 0, 0, 0)),
            pl.BlockSpec((C_CHUNK, C_GROUPS), lambda bi, i: (0, 0)),
            pl.BlockSpec((None, cw, d), lambda bi, i: (layer, 0, 0), pipeline_mode=pl.Buffered(1)),
            pl.BlockSpec((None, tm, d), lambda bi, i: (bi, i, 0)),
            row_vec,
            pl.BlockSpec((1, d), lambda bi, i: (0, 0)),
            row_vec,
            row_vec,
        ],
        out_specs=out_specs,
        out_shape=out_shape,
        scratch_shapes=[pltpu.VMEM((tm, cw), BF16)],
        compiler_params=pltpu.CompilerParams(
            dimension_semantics=("arbitrary", "arbitrary"),
            vmem_limit_bytes=VMEM_LIMIT),
        name="sgu_final" if final else "sgu",
    )(proj, proj, proj, ln_g.reshape(1, cw), ln_b.reshape(1, cw), ws_bf16, bs_t, w_bf16, x,
      gate, next_g.reshape(1, d), next_shift, next_scale)


def _rope_tables(s):
    half = HEAD_DIM // 2
    inv = ROPE_THETA ** (-jnp.arange(half, dtype=F32) / half)
    ang = jnp.arange(s, dtype=F32)[:, None] * inv[None, :]
    cos = jnp.cos(ang)
    sin = jnp.sin(ang)
    return (jnp.concatenate([cos, cos], axis=-1), jnp.concatenate([-sin, sin], axis=-1))


def _split_mod(mod, batch):
    m = mod[:batch].reshape(batch, 1, 3, D_MODEL)
    return m[:, :, 0, :], m[:, :, 1, :], m[:, :, 2, :]


def kernel(x, c, ab_norm_g, ab_w_mod, ab_b_mod, ab_w_in, ab_conv_w, ab_w_out, sg_norm_g, sg_w_mod, sg_b_mod, sg_w_in, sg_ln_g, sg_ln_b, sg_w_s, sg_b_s, sg_w_out, final_norm_g):
    batch, s, _ = x.shape
    depth = ab_w_in.shape[0] + sg_w_in.shape[0]
    c_pad = jnp.pad(c, ((0, 16 - batch), (0, 0)))
    mod_ab = _modulation(c_pad, ab_w_mod, ab_b_mod)
    mod_sg = _modulation(c_pad, sg_w_mod, sg_b_mod)
    cos, sin = _rope_tables(s)

    ab_w_in_b = ab_w_in.astype(BF16)
    ab_w_out_b = ab_w_out.astype(BF16)
    sg_w_in_b = sg_w_in.astype(BF16)
    sg_w_out_b = sg_w_out.astype(BF16)
    sg_w_s_b = sg_w_s.astype(BF16)

    params = []
    for layer in range(depth):
        i = layer // 2
        norm_g, mod = (ab_norm_g, mod_ab) if layer % 2 == 0 else (sg_norm_g, mod_sg)
        params.append((norm_g[i],) + _split_mod(mod[i], batch))

    h = None
    for layer in range(depth):
        i = layer // 2
        norm_g, shift, scale, gate = params[layer]
        final = layer == depth - 1
        nxt = (final_norm_g, gate, gate) if final else params[layer + 1][:3]
        src, normalize = (x, True) if h is None else (h, False)
        if layer % 2 == 0:
            proj = _in_projection(src, norm_g, shift, scale, ab_w_in_b, i, cos, sin,
                                  epilogues=AB_EPILOGUES, normalize=normalize)
            ya, yb = _attn_conv(proj, ab_conv_w[i])
            res = _ab_out_projection(ya, yb, ab_w_out_b, i, x, gate, *nxt, final=final)
        else:
            proj = _in_projection(src, norm_g, shift, scale, sg_w_in_b, i, cos, sin,
                                  epilogues=SG_EPILOGUES, normalize=normalize)
            res = _sgu(proj, sg_ln_g[i], sg_ln_b[i], sg_w_s_b, sg_b_s[i].T, sg_w_out_b, i, x, gate,
                       *nxt, final=final)
        if final:
            return res
        x, h = res
```

```python
import functools

import jax
import jax.numpy as jnp
from jax import lax
from jax.experimental import pallas as pl
from jax.experimental.pallas import tpu as pltpu

F32 = jnp.float32
BF16 = jnp.bfloat16

D_MODEL = 2048
HEAD_DIM = 128
A_WIDTH = 1024
A_HEADS = 8
B_WIDTH = 1024
DILATIONS = (1, 4, 16)
RADIUS = 64
ROPE_THETA = 10000.0
NEG_INF = -1e30
C_WIDTH = 2048
C_GROUPS = 8
C_CHUNK = 128
EPS = 1e-6

Q_SCALE = HEAD_DIM ** -0.5 * 1.4426950408889634
Q_TILE = 128
K_WIN = Q_TILE + 2 * RADIUS
TILE_GROUP = 32
VMEM_LIMIT = 60 * 1024 * 1024

AB_EPILOGUES = (("rope_q", "rope_k"), ("none", "silu"), ("none", "none"), ("none", "silu"))
SG_EPILOGUES = (("gelu", "gelu"), ("gelu", "gelu"), ("silu", "silu"))


def _silu(z):
    hz = 0.5 * z
    return hz + hz * jnp.tanh(hz)


def _gelu_tanh(x):
    c = 0.7978845608028654
    return x * (0.5 * (1.0 + jnp.tanh(c * (x + 0.044715 * (x * x * x)))))


def _rms_norm(x, gain):
    ms = jnp.mean(x * x, axis=-1, keepdims=True)
    return x * lax.rsqrt(ms + EPS) * gain


def _modulated_rms_norm(x, gm, shift):
    return _rms_norm(x, gm) + shift


def _residual_tail(xn, rows, ng_ref, nshift_ref, nscale_ref, o_ref, h_ref, final):
    if final:
        o_ref[rows, :] = _rms_norm(xn, ng_ref[...])
    else:
        o_ref[rows, :] = xn
        gm = ng_ref[...] * (1.0 + nscale_ref[...])
        h_ref[rows, :] = _modulated_rms_norm(xn, gm, nshift_ref[...]).astype(BF16)


def _mod_kernel(c_ref, w_ref, b_ref, o_ref):
    a = _silu(c_ref[...]).astype(BF16)
    w = w_ref[...].astype(BF16)
    o_ref[...] = jnp.dot(a, w, preferred_element_type=F32) + b_ref[...]


def _modulation(c_pad, w_mod, b_mod, tn=1024):
    n_layers, d, n = w_mod.shape
    rows = c_pad.shape[0]
    return pl.pallas_call(
        _mod_kernel,
        grid=(n_layers, n // tn),
        in_specs=[
            pl.BlockSpec((rows, d), lambda l, j: (0, 0)),
            pl.BlockSpec((None, d, tn), lambda l, j: (l, 0, j)),
            pl.BlockSpec((None, 1, tn), lambda l, j: (l, 0, j)),
        ],
        out_specs=pl.BlockSpec((None, rows, tn), lambda l, j: (l, 0, j)),
        out_shape=jax.ShapeDtypeStruct((n_layers, rows, n), F32),
        compiler_params=pltpu.CompilerParams(
            dimension_semantics=("arbitrary", "arbitrary"),
            vmem_limit_bytes=VMEM_LIMIT),
        name="modulation",
    )(c_pad, w_mod, b_mod.reshape(n_layers, 1, n))


def _inproj_kernel(x_ref, g_ref, shift_ref, scale_ref, w_ref, cos_ref, sin_ref, o_ref,
                   *scratch, epilogues, seg, normalize, row_chunk, out_chunk):
    j = pl.program_id(2)
    tm = x_ref.shape[0]

    if normalize:
        h_ref, = scratch

        @pl.when(j == 0)
        def _():
            gm = g_ref[...] * (1.0 + scale_ref[...])
            shift = shift_ref[...]

            def body(c, carry):
                rows = pl.ds(pl.multiple_of(c * row_chunk, row_chunk), row_chunk)
                h_ref[rows, :] = _modulated_rms_norm(x_ref[rows, :], gm, shift).astype(BF16)
                return carry

            lax.fori_loop(0, tm // row_chunk, body, 0, unroll=4)
    else:
        h_ref = x_ref

    def rope(acc, c0, f):
        cos = cos_ref[...] * f
        sin = sin_ref[...] * f
        for hh in range(seg // HEAD_DIM):
            cols = slice(c0 + hh * HEAD_DIM, c0 + (hh + 1) * HEAD_DIM)
            blk = acc[:, cols]
            o_ref[:, cols] = (blk * cos + pltpu.roll(blk, HEAD_DIM // 2, 1) * sin).astype(BF16)

    def pointwise(acc, c0, fn):
        for r0 in range(0, tm, out_chunk):
            o_ref[r0:r0 + out_chunk, c0:c0 + seg] = fn(
                acc[r0:r0 + out_chunk, c0:c0 + seg]).astype(BF16)

    finish = {
        "rope_q": lambda acc, c0: rope(acc, c0, Q_SCALE),
        "rope_k": lambda acc, c0: rope(acc, c0, 1.0),
        "none": lambda acc, c0: pointwise(acc, c0, lambda t: t),
        "silu": lambda acc, c0: pointwise(acc, c0, _silu),
        "gelu": lambda acc, c0: pointwise(acc, c0, _gelu_tanh),
    }
    for kinds in sorted(set(epilogues)):
        tiles = [t for t, e in enumerate(epilogues) if e == kinds]
        cond = functools.reduce(jnp.logical_or, [j == t for t in tiles])

        @pl.when(cond)
        def _(kinds=kinds):
            acc = jnp.dot(h_ref[...], w_ref[...], preferred_element_type=F32)
            for si, kind in enumerate(kinds):
                finish[kind](acc, si * seg)


def _in_projection(x, norm_g, shift, scale, w_bf16, layer, cos, sin, *, epilogues, normalize,
                   tm=1024):
    b, s, d = x.shape
    n = w_bf16.shape[2]
    tn = n // len(epilogues)
    seg = tn // len(epilogues[0])
    kern = functools.partial(_inproj_kernel, epilogues=epilogues, seg=seg, normalize=normalize,
                             row_chunk=16, out_chunk=256)
    return pl.pallas_call(
        kern,
        grid=(b, s // tm, n // tn),
        in_specs=[
            pl.BlockSpec((None, tm, d), lambda bi, i, j: (bi, i, 0)),
            pl.BlockSpec((1, d), lambda bi, i, j: (0, 0)),
            pl.BlockSpec((None, 1, d), lambda bi, i, j: (bi, 0, 0)),
            pl.BlockSpec((None, 1, d), lambda bi, i, j: (bi, 0, 0)),
            pl.BlockSpec((None, d, tn), lambda bi, i, j: (layer, 0, j)),
            pl.BlockSpec((tm, HEAD_DIM), lambda bi, i, j: (i, 0)),
            pl.BlockSpec((tm, HEAD_DIM), lambda bi, i, j: (i, 0)),
        ],
        out_specs=pl.BlockSpec((None, tm, tn), lambda bi, i, j: (bi, i, j)),
        out_shape=jax.ShapeDtypeStruct((b, s, n), BF16),
        scratch_shapes=[pltpu.VMEM((tm, d), BF16)] if normalize else [],
        compiler_params=pltpu.CompilerParams(
            dimension_semantics=("arbitrary", "arbitrary", "arbitrary"),
            vmem_limit_bytes=VMEM_LIMIT),
        name="in_projection_ab" if "rope_q" in epilogues[0] else "in_projection_sg",
    )(x, norm_g.reshape(1, d), shift, scale, w_bf16, cos, sin)


def _attn_conv_kernel(q_ref, k_ref, v_ref, za_ref, ub_ref, gb_ref, gc_ref, zb_ref, cw_ref,
                      ya_ref, yb_ref,
                      fa, fb, q4, k4, v4, q16, k16, v16, v1, mask_scr, pbuf):
    s = q_ref.shape[0]
    d4, d16 = DILATIONS[1], DILATIONS[2]
    n4, n16 = s // d4, s // d16
    step = d16 // d4
    chunk = 256

    qi = lax.broadcasted_iota(jnp.int32, (Q_TILE, K_WIN), 0)
    ki = lax.broadcasted_iota(jnp.int32, (Q_TILE, K_WIN), 1)
    for t in range(3):
        mask_scr[t] = jnp.where(jnp.abs(qi + t * RADIUS - ki) <= RADIUS, 0.0, NEG_INF)

    ones = jnp.ones((s, HEAD_DIM), BF16)
    v1[:, HEAD_DIM:2 * HEAD_DIM] = ones
    v4[:, HEAD_DIM:2 * HEAD_DIM] = ones
    v16[:, HEAD_DIM:2 * HEAD_DIM] = ones
    v1[:, 0:HEAD_DIM] = v_ref[...]

    for idx, (src, dst4, dst16) in enumerate(((q_ref, q4, q16), (k_ref, k4, k16),
                                              (v_ref, v4, v16))):
        for c0 in range(0, s, chunk):
            fa[idx, c0:c0 + chunk, :] = src[c0:c0 + chunk, :].astype(F32)
        for r in range(d4):
            for c0 in range(0, n4, chunk):
                part = fa[idx, pl.ds(r + d4 * c0, chunk, stride=d4), :]
                fb[idx, r * n4 + c0:r * n4 + c0 + chunk, :] = part
                dst4[r * n4 + c0:r * n4 + c0 + chunk, 0:HEAD_DIM] = part.astype(BF16)
        for r in range(d4):
            for a in range(step):
                part = fb[idx, pl.ds(r * n4 + a, n16, stride=step), :]
                r16 = r + d4 * a
                dst16[r16 * n16:(r16 + 1) * n16, 0:HEAD_DIM] = part.astype(BF16)

    patterns = ((d16, q16, k16, v16), (d4, q4, k4, v4), (1, q_ref, k_ref, v1))
    states = (None, fa, fb)
    for p, (dil, qs, ks, vs) in enumerate(patterns):
        n = s // dil
        tiles_per_seg = n // Q_TILE

        def one_tile(t, qs=qs, ks=ks, vs=vs, dil=dil, n=n, tiles_per_seg=tiles_per_seg, p=p):
            seg = t // tiles_per_seg
            l0 = (t % tiles_per_seg) * Q_TILE
            kstart = jnp.clip(l0 - RADIUS, 0, n - K_WIN)
            which = (l0 - kstart) // RADIUS
            row0 = pl.multiple_of(seg * n + l0, Q_TILE)
            krow0 = pl.multiple_of(seg * n + kstart, RADIUS)
            rows = pl.ds(row0, Q_TILE)
            qt = qs[rows, :]
            kt = ks[pl.ds(krow0, K_WIN), :]
            vt = vs[pl.ds(krow0, K_WIN), :]
            sc = lax.dot_general(qt, kt, (((1,), (1,)), ((), ())),
                                 preferred_element_type=F32)
            sc = sc + mask_scr[which]
            m = jnp.max(sc, axis=-1, keepdims=True)
            e = jnp.exp2(sc - m).astype(BF16)
            ov = jnp.dot(e, vt, preferred_element_type=F32)
            num = ov[:, 0:HEAD_DIM]
            den = ov[:, HEAD_DIM:2 * HEAD_DIM]
            m = jnp.broadcast_to(m, (Q_TILE, HEAD_DIM))
            if p > 0:
                prev = states[p]
                m_prev = prev[1, rows, :]
                m_all = jnp.maximum(m_prev, m)
                w_prev = jnp.exp2(m_prev - m_all)
                w_cur = jnp.exp2(m - m_all)
                num = w_prev * prev[0, rows, :] + w_cur * num
                den = w_prev * prev[2, rows, :] + w_cur * den
                m = m_all
            if p + 1 < len(patterns):
                nxt, dil_next = states[p + 1], patterns[p + 1][0]
                dst0 = (seg % dil_next) * (s // dil_next) + (dil // dil_next) * l0 + seg // dil_next
                dst = pl.ds(dst0, Q_TILE, stride=dil // dil_next)
                nxt[0, dst, :] = num
                nxt[1, dst, :] = m
                nxt[2, dst, :] = den
            else:
                ya_ref[rows, :] = (num / den * za_ref[rows, :].astype(F32)).astype(BF16)

        def group_body(gi, carry, one_tile=one_tile):
            for u in range(TILE_GROUP):
                one_tile(gi * TILE_GROUP + u)
            return carry

        lax.fori_loop(0, s // (Q_TILE * TILE_GROUP), group_body, 0)

    zeros8 = jnp.zeros((8, HEAD_DIM), F32)
    pbuf[0:8, :] = zeros8
    pbuf[8 + s:16 + s, :] = zeros8
    for c0 in range(0, s, chunk):
        rows = slice(c0, c0 + chunk)
        pbuf[8 + c0:8 + c0 + chunk, :] = gc_ref[rows, :].astype(F32) * ub_ref[rows, :].astype(F32)
    w0 = cw_ref[0:1, :]
    w1 = cw_ref[1:2, :]
    w2 = cw_ref[2:3, :]
    for c0 in range(0, s, chunk):
        conv = (w0 * pbuf[7 + c0:7 + c0 + chunk, :] + w1 * pbuf[8 + c0:8 + c0 + chunk, :]
                + w2 * pbuf[9 + c0:9 + c0 + chunk, :])
        gate = gb_ref[c0:c0 + chunk, :].astype(F32) * zb_ref[c0:c0 + chunk, :].astype(F32)
        yb_ref[c0:c0 + chunk, :] = (gate * conv).astype(BF16)


def _attn_conv(proj, conv_w):
    b, s, _ = proj.shape
    nblk = A_WIDTH // HEAD_DIM

    def col(slot):
        return pl.BlockSpec((None, s, HEAD_DIM), lambda bi, h: (bi, 0, slot * nblk + h))

    out_spec = pl.BlockSpec((None, s, HEAD_DIM), lambda bi, h: (bi, 0, h))
    return pl.pallas_call(
        _attn_conv_kernel,
        grid=(b, A_HEADS),
        in_specs=[col(0), col(1), col(2), col(3), col(4), col(5), col(6), col(7),
                  pl.BlockSpec((3, HEAD_DIM), lambda bi, h: (0, h))],
        out_specs=[out_spec, out_spec],
        out_shape=[jax.ShapeDtypeStruct((b, s, A_WIDTH), BF16),
                   jax.ShapeDtypeStruct((b, s, B_WIDTH), BF16)],
        scratch_shapes=[
            pltpu.VMEM((3, s, HEAD_DIM), F32), pltpu.VMEM((3, s, HEAD_DIM), F32),
            pltpu.VMEM((s, HEAD_DIM), BF16), pltpu.VMEM((s, HEAD_DIM), BF16),
            pltpu.VMEM((s, 2 * HEAD_DIM), BF16),
            pltpu.VMEM((s, HEAD_DIM), BF16), pltpu.VMEM((s, HEAD_DIM), BF16),
            pltpu.VMEM((s, 2 * HEAD_DIM), BF16),
            pltpu.VMEM((s, 2 * HEAD_DIM), BF16),
            pltpu.VMEM((3, Q_TILE, K_WIN), F32),
            pltpu.VMEM((s + 16, HEAD_DIM), F32),
        ],
        compiler_params=pltpu.CompilerParams(
            dimension_semantics=("arbitrary", "arbitrary"),
            vmem_limit_bytes=VMEM_LIMIT),
        name="attn_conv",
    )(proj, proj, proj, proj, proj, proj, proj, proj, conv_w)


def _ab_out_kernel(ya_ref, yb_ref, w_ref, x_ref, gate_ref, ng_ref, nshift_ref, nscale_ref,
                   o_ref, *h_out, final, dot_rows):
    h_ref = None if final else h_out[0]
    for d0 in range(0, x_ref.shape[0], dot_rows):
        rows = slice(d0, d0 + dot_rows)
        out = jnp.dot(ya_ref[rows, :], w_ref[0:A_WIDTH, :], preferred_element_type=F32)
        out = out + jnp.dot(yb_ref[rows, :], w_ref[A_WIDTH:A_WIDTH + B_WIDTH, :],
                            preferred_element_type=F32)
        xn = x_ref[rows, :] + gate_ref[...] * out
        _residual_tail(xn, rows, ng_ref, nshift_ref, nscale_ref, o_ref, h_ref, final)


def _residual_out(b, s, d, tm, final):
    spec = pl.BlockSpec((None, tm, d), lambda bi, i: (bi, i, 0))
    if final:
        return spec, jax.ShapeDtypeStruct((b, s, d), F32)
    return [spec, spec], [jax.ShapeDtypeStruct((b, s, d), F32),
                          jax.ShapeDtypeStruct((b, s, d), BF16)]


def _ab_out_projection(ya, yb, w_bf16, layer, x, gate, next_g, next_shift, next_scale, *, final,
                       tm=512):
    b, s, d = x.shape
    out_specs, out_shape = _residual_out(b, s, d, tm, final)
    row_vec = pl.BlockSpec((None, 1, d), lambda bi, i: (bi, 0, 0))
    return pl.pallas_call(
        functools.partial(_ab_out_kernel, final=final, dot_rows=256),
        grid=(b, s // tm),
        in_specs=[
            pl.BlockSpec((None, tm, A_WIDTH), lambda bi, i: (bi, i, 0)),
            pl.BlockSpec((None, tm, B_WIDTH), lambda bi, i: (bi, i, 0)),
            pl.BlockSpec((None, A_WIDTH + B_WIDTH, d), lambda bi, i: (layer, 0, 0),
                         pipeline_mode=pl.Buffered(1)),
            pl.BlockSpec((None, tm, d), lambda bi, i: (bi, i, 0)),
            row_vec,
            pl.BlockSpec((1, d), lambda bi, i: (0, 0)),
            row_vec,
            row_vec,
        ],
        out_specs=out_specs,
        out_shape=out_shape,
        compiler_params=pltpu.CompilerParams(
            dimension_semantics=("arbitrary", "arbitrary"),
            vmem_limit_bytes=VMEM_LIMIT),
        name="ab_out_projection",
    )(ya, yb, w_bf16, x, gate, next_g.reshape(1, d), next_shift, next_scale)


def _sgu_kernel(u_ref, v_ref, z_ref, lng_ref, lnb_ref, ws_ref, bs_ref, w_ref, x_ref, gate_ref,
                ng_ref, nshift_ref, nscale_ref, o_ref, *rest, final, dot_rows):
    h_ref, y_scr = (None, rest[0]) if final else rest
    tm = u_ref.shape[0]
    gw = C_WIDTH // C_GROUPS
    lng = lng_ref[...]
    lnb = lnb_ref[...]
    for d0 in range(0, tm, dot_rows):
        for c0 in range(d0, d0 + dot_rows, C_CHUNK):
            rows = slice(c0, c0 + C_CHUNK)
            v = v_ref[rows, :].astype(F32)
            mu = jnp.mean(v, axis=-1, keepdims=True)
            vc = v - mu
            var = jnp.mean(vc * vc, axis=-1, keepdims=True)
            vn = (vc * lax.rsqrt(var + EPS) * lng + lnb).astype(BF16)
            for g in range(C_GROUPS):
                cols = slice(g * gw, (g + 1) * gw)
                mixed = jnp.dot(ws_ref[g], vn[:, cols], preferred_element_type=F32)
                mixed = mixed + bs_ref[:, g:g + 1]
                gated = u_ref[rows, cols].astype(F32) * z_ref[rows, cols].astype(F32)
                y_scr[rows, cols] = (gated * mixed).astype(BF16)
        rows = slice(d0, d0 + dot_rows)
        out = jnp.dot(y_scr[rows, :], w_ref[...], preferred_element_type=F32)
        xn = x_ref[rows, :] + gate_ref[...] * out
        _residual_tail(xn, rows, ng_ref, nshift_ref, nscale_ref, o_ref, h_ref, final)


def _sgu(proj, ln_g, ln_b, ws_bf16, bs_t, w_bf16, layer, x, gate, next_g, next_shift, next_scale, *,
         final, tm=512):
    b, s, d = x.shape
    cw = C_WIDTH
    kern = functools.partial(_sgu_kernel, final=final, dot_rows=256)
    out_specs, out_shape = _residual_out(b, s, d, tm, final)
    row_vec = pl.BlockSpec((None, 1, d), lambda bi, i: (bi, 0, 0))
    return pl.pallas_call(
        kern,
        grid=(b, s // tm),
        in_specs=[
            pl.BlockSpec((None, tm, cw), lambda bi, i: (bi, i, 0)),
            pl.BlockSpec((None, tm, cw), lambda bi, i: (bi, i, 1)),
            pl.BlockSpec((None, tm, cw), lambda bi, i: (bi, i, 2)),
            pl.BlockSpec((1, cw), lambda bi, i: (0, 0)),
            pl.BlockSpec((1, cw), lambda bi, i: (0, 0)),
            pl.BlockSpec((None, C_GROUPS, C_CHUNK, C_CHUNK), lambda bi, i: (layer, 0, 0, 0)),
            pl.BlockSpec((C_CHUNK, C_GROUPS), lambda bi, i: (0, 0)),
            pl.BlockSpec((None, cw, d), lambda bi, i: (layer, 0, 0), pipeline_mode=pl.Buffered(1)),
            pl.BlockSpec((None, tm, d), lambda bi, i: (bi, i, 0)),
            row_vec,
            pl.BlockSpec((1, d), lambda bi, i: (0, 0)),
            row_vec,
            row_vec,
        ],
        out_specs=out_specs,
        out_shape=out_shape,
        scratch_shapes=[pltpu.VMEM((tm, cw), BF16)],
        compiler_params=pltpu.CompilerParams(
            dimension_semantics=("arbitrary", "arbitrary"),
            vmem_limit_bytes=VMEM_LIMIT),
        name="sgu_final" if final else "sgu",
    )(proj, proj, proj, ln_g.reshape(1, cw), ln_b.reshape(1, cw), ws_bf16, bs_t, w_bf16, x,
      gate, next_g.reshape(1, d), next_shift, next_scale)


def _rope_tables(s):
    half = HEAD_DIM // 2
    inv = ROPE_THETA ** (-jnp.arange(half, dtype=F32) / half)
    ang = jnp.arange(s, dtype=F32)[:, None] * inv[None, :]
    cos = jnp.cos(ang)
    sin = jnp.sin(ang)
    return (jnp.concatenate([cos, cos], axis=-1), jnp.concatenate([-sin, sin], axis=-1))


def _split_mod(mod, batch):
    m = mod[:batch].reshape(batch, 1, 3, D_MODEL)
    return m[:, :, 0, :], m[:, :, 1, :], m[:, :, 2, :]


def kernel(x, c, ab_norm_g, ab_w_mod, ab_b_mod, ab_w_in, ab_conv_w, ab_w_out, sg_norm_g, sg_w_mod, sg_b_mod, sg_w_in, sg_ln_g, sg_ln_b, sg_w_s, sg_b_s, sg_w_out, final_norm_g):
    batch, s, _ = x.shape
    depth = ab_w_in.shape[0] + sg_w_in.shape[0]
    c_pad = jnp.pad(c, ((0, 16 - batch), (0, 0)))
    mod_ab = _modulation(c_pad, ab_w_mod, ab_b_mod)
    mod_sg = _modulation(c_pad, sg_w_mod, sg_b_mod)
    cos, sin = _rope_tables(s)

    ab_w_in_b = ab_w_in.astype(BF16)
    ab_w_out_b = ab_w_out.astype(BF16)
    sg_w_in_b = sg_w_in.astype(BF16)
    sg_w_out_b = sg_w_out.astype(BF16)
    sg_w_s_b = sg_w_s.astype(BF16)

    params = []
    for layer in range(depth):
        i = layer // 2
        norm_g, mod = (ab_norm_g, mod_ab) if layer % 2 == 0 else (sg_norm_g, mod_sg)
        params.append((norm_g[i],) + _split_mod(mod[i], batch))

    h = None
    for layer in range(depth):
        i = layer // 2
        norm_g, shift, scale, gate = params[layer]
        final = layer == depth - 1
        nxt = (final_norm_g, gate, gate) if final else params[layer + 1][:3]
        src, normalize = (x, True) if h is None else (h, False)
        if layer % 2 == 0:
            proj = _in_projection(src, norm_g, shift, scale, ab_w_in_b, i, cos, sin,
                                  epilogues=AB_EPILOGUES, normalize=normalize)
            ya, yb = _attn_conv(proj, ab_conv_w[i])
            res = _ab_out_projection(ya, yb, ab_w_out_b, i, x, gate, *nxt, final=final)
        else:
            proj = _in_projection(src, norm_g, shift, scale, sg_w_in_b, i, cos, sin,
                                  epilogues=SG_EPILOGUES, normalize=normalize)
            res = _sgu(proj, sg_ln_g[i], sg_ln_b[i], sg_w_s_b, sg_b_s[i].T, sg_w_out_b, i, x, gate,
                       *nxt, final=final)
        if final:
            return res
        x, h = res
```

```python
import functools

import jax
import jax.numpy as jnp
from jax import lax
from jax.experimental import pallas as pl
from jax.experimental.pallas import tpu as pltpu

F32 = jnp.float32
BF16 = jnp.bfloat16

D_MODEL = 2048
HEAD_DIM = 128
A_WIDTH = 1024
A_HEADS = 8
B_WIDTH = 1024
B_CONV = 3
DILATIONS = (1, 4, 16)
RADIUS = 64
ROPE_THETA = 10000.0
NEG_INF = -1e30
C_WIDTH = 2048
C_GROUPS = 8
C_CHUNK = 128
EPS = 1e-6

Q_SCALE = HEAD_DIM ** -0.5 * 1.4426950408889634
Q_TILE = 128
K_WIN = Q_TILE + 2 * RADIUS
TILE_GROUP = 32
VMEM_LIMIT = 60 * 1024 * 1024

AB_EPILOGUES = (("rope_q", "rope_k"), ("none", "silu"), ("none", "none"), ("none", "silu"))
SG_EPILOGUES = (("gelu", "gelu"), ("gelu", "gelu"), ("silu", "silu"))


def _silu(z):
    hz = 0.5 * z
    return hz + hz * jnp.tanh(hz)


def _gelu_tanh(x):
    c = 0.7978845608028654
    return x * (0.5 * (1.0 + jnp.tanh(c * (x + 0.044715 * (x * x * x)))))


def _rms_norm(x, gain):
    ms = jnp.mean(x * x, axis=-1, keepdims=True)
    return x * lax.rsqrt(ms + EPS) * gain


def _modulated_rms_norm(x, gm, shift):
    return _rms_norm(x, gm) + shift


def _row_groups(tm):
    edges = (0, tm // 2, tm)
    return tuple(zip(edges[:-1], edges[1:]))


def _residual_tail(xn, rows, ng_ref, nshift_ref, nscale_ref, o_ref, h_ref, final):
    if final:
        o_ref[rows, :] = _rms_norm(xn, ng_ref[...])
    else:
        o_ref[rows, :] = xn
        gm = ng_ref[...] * (1.0 + nscale_ref[...])
        h_ref[rows, :] = _modulated_rms_norm(xn, gm, nshift_ref[...]).astype(BF16)


def _mod_kernel(c_ref, w_ref, b_ref, o_ref):
    a = _silu(c_ref[...]).astype(BF16)
    w = w_ref[...].astype(BF16)
    o_ref[...] = jnp.dot(a, w, preferred_element_type=F32) + b_ref[...]


def _modulation(c_pad, w_mod, b_mod, tn=1024):
    n_layers, d, n = w_mod.shape
    rows = c_pad.shape[0]
    return pl.pallas_call(
        _mod_kernel,
        grid=(n_layers, n // tn),
        in_specs=[
            pl.BlockSpec((rows, d), lambda l, j: (0, 0)),
            pl.BlockSpec((None, d, tn), lambda l, j: (l, 0, j)),
            pl.BlockSpec((None, 1, tn), lambda l, j: (l, 0, j)),
        ],
        out_specs=pl.BlockSpec((None, rows, tn), lambda l, j: (l, 0, j)),
        out_shape=jax.ShapeDtypeStruct((n_layers, rows, n), F32),
        compiler_params=pltpu.CompilerParams(
            dimension_semantics=("arbitrary", "arbitrary"),
            vmem_limit_bytes=VMEM_LIMIT),
        name="modulation",
    )(c_pad, w_mod, b_mod.reshape(n_layers, 1, n))


def _inproj_kernel(x_ref, g_ref, shift_ref, scale_ref, w_ref, cos_ref, sin_ref, o_ref,
                   *scratch, epilogues, seg, normalize, row_groups, row_chunk, out_chunk):
    j = pl.program_id(2)
    tm = x_ref.shape[0]

    if normalize:
        h_ref, = scratch

        @pl.when(j == 0)
        def _():
            gm = g_ref[...] * (1.0 + scale_ref[...])
            shift = shift_ref[...]

            def body(c, carry):
                rows = pl.ds(pl.multiple_of(c * row_chunk, row_chunk), row_chunk)
                h_ref[rows, :] = _modulated_rms_norm(x_ref[rows, :], gm, shift).astype(BF16)
                return carry

            lax.fori_loop(0, tm // row_chunk, body, 0, unroll=4)
    else:
        h_ref = x_ref

    def rope(acc, r0, c0, f):
        rows = slice(r0, r0 + acc.shape[0])
        cos = cos_ref[rows, :] * f
        sin = sin_ref[rows, :] * f
        for hh in range(seg // HEAD_DIM):
            cols = slice(c0 + hh * HEAD_DIM, c0 + (hh + 1) * HEAD_DIM)
            blk = acc[:, cols]
            o_ref[rows, cols] = (blk * cos + pltpu.roll(blk, HEAD_DIM // 2, 1) * sin).astype(BF16)

    def pointwise(acc, r0, c0, fn):
        for q0 in range(0, acc.shape[0], out_chunk):
            o_ref[r0 + q0:r0 + q0 + out_chunk, c0:c0 + seg] = fn(
                acc[q0:q0 + out_chunk, c0:c0 + seg]).astype(BF16)

    finish = {
        "rope_q": lambda acc, r0, c0: rope(acc, r0, c0, Q_SCALE),
        "rope_k": lambda acc, r0, c0: rope(acc, r0, c0, 1.0),
        "none": lambda acc, r0, c0: pointwise(acc, r0, c0, lambda t: t),
        "silu": lambda acc, r0, c0: pointwise(acc, r0, c0, _silu),
        "gelu": lambda acc, r0, c0: pointwise(acc, r0, c0, _gelu_tanh),
    }
    for kinds in sorted(set(epilogues)):
        tiles = [t for t, e in enumerate(epilogues) if e == kinds]
        cond = functools.reduce(jnp.logical_or, [j == t for t in tiles])

        @pl.when(cond)
        def _(kinds=kinds):
            for r0, r1 in row_groups:
                acc = jnp.dot(h_ref[r0:r1, :], w_ref[...], preferred_element_type=F32)
                for si, kind in enumerate(kinds):
                    finish[kind](acc, r0, si * seg)


def _in_projection(x, norm_g, shift, scale, w_bf16, layer, cos, sin, *, epilogues, normalize,
                   tm=1024):
    b, s, d = x.shape
    n = w_bf16.shape[2]
    tn = n // len(epilogues)
    seg = tn // len(epilogues[0])
    kern = functools.partial(_inproj_kernel, epilogues=epilogues, seg=seg, normalize=normalize,
                             row_groups=((0, tm),), row_chunk=16, out_chunk=256)
    return pl.pallas_call(
        kern,
        grid=(b, s // tm, n // tn),
        in_specs=[
            pl.BlockSpec((None, tm, d), lambda bi, i, j: (bi, i, 0)),
            pl.BlockSpec((1, d), lambda bi, i, j: (0, 0)),
            pl.BlockSpec((None, 1, d), lambda bi, i, j: (bi, 0, 0)),
            pl.BlockSpec((None, 1, d), lambda bi, i, j: (bi, 0, 0)),
            pl.BlockSpec((None, d, tn), lambda bi, i, j: (layer, 0, j)),
            pl.BlockSpec((tm, HEAD_DIM), lambda bi, i, j: (i, 0)),
            pl.BlockSpec((tm, HEAD_DIM), lambda bi, i, j: (i, 0)),
        ],
        out_specs=pl.BlockSpec((None, tm, tn), lambda bi, i, j: (bi, i, j)),
        out_shape=jax.ShapeDtypeStruct((b, s, n), BF16),
        scratch_shapes=[pltpu.VMEM((tm, d), BF16)] if normalize else [],
        compiler_params=pltpu.CompilerParams(
            dimension_semantics=("arbitrary", "arbitrary", "arbitrary"),
            vmem_limit_bytes=VMEM_LIMIT),
        name="in_projection_ab" if "rope_q" in epilogues[0] else "in_projection_sg",
    )(x, norm_g.reshape(1, d), shift, scale, w_bf16, cos, sin)


def _attention_kernel(q_ref, k_ref, v_ref, za_ref, ya_ref,
                      fa, fb, q4, k4, v4, q16, k16, v16, v1, mask_scr):
    s = q_ref.shape[0]
    d4, d16 = DILATIONS[1], DILATIONS[2]
    n4, n16 = s // d4, s // d16
    step = d16 // d4
    chunk = 256

    qi = lax.broadcasted_iota(jnp.int32, (Q_TILE, K_WIN), 0)
    ki = lax.broadcasted_iota(jnp.int32, (Q_TILE, K_WIN), 1)
    for t in range(3):
        mask_scr[t] = jnp.where(jnp.abs(qi + t * RADIUS - ki) <= RADIUS, 0.0, NEG_INF)

    ones = jnp.ones((s, HEAD_DIM), BF16)
    v1[:, HEAD_DIM:2 * HEAD_DIM] = ones
    v4[:, HEAD_DIM:2 * HEAD_DIM] = ones
    v16[:, HEAD_DIM:2 * HEAD_DIM] = ones
    v1[:, 0:HEAD_DIM] = v_ref[...]

    for idx, (src, dst4, dst16) in enumerate(((q_ref, q4, q16), (k_ref, k4, k16),
                                              (v_ref, v4, v16))):
        for c0 in range(0, s, chunk):
            fa[idx, c0:c0 + chunk, :] = src[c0:c0 + chunk, :].astype(F32)
        for r in range(d4):
            for c0 in range(0, n4, chunk):
                part = fa[idx, pl.ds(r + d4 * c0, chunk, stride=d4), :]
                fb[idx, r * n4 + c0:r * n4 + c0 + chunk, :] = part
                dst4[r * n4 + c0:r * n4 + c0 + chunk, 0:HEAD_DIM] = part.astype(BF16)
        for r in range(d4):
            for a in range(step):
                part = fb[idx, pl.ds(r * n4 + a, n16, stride=step), :]
                r16 = r + d4 * a
                dst16[r16 * n16:(r16 + 1) * n16, 0:HEAD_DIM] = part.astype(BF16)

    patterns = ((d16, q16, k16, v16), (d4, q4, k4, v4), (1, q_ref, k_ref, v1))
    states = (None, fa, fb)
    for p, (dil, qs, ks, vs) in enumerate(patterns):
        n = s // dil
        tiles_per_seg = n // Q_TILE

        def one_tile(t, qs=qs, ks=ks, vs=vs, dil=dil, n=n, tiles_per_seg=tiles_per_seg, p=p):
            seg = t // tiles_per_seg
            l0 = (t % tiles_per_seg) * Q_TILE
            kstart = jnp.clip(l0 - RADIUS, 0, n - K_WIN)
            which = (l0 - kstart) // RADIUS
            row0 = pl.multiple_of(seg * n + l0, Q_TILE)
            krow0 = pl.multiple_of(seg * n + kstart, RADIUS)
            rows = pl.ds(row0, Q_TILE)
            qt = qs[rows, :]
            kt = ks[pl.ds(krow0, K_WIN), :]
            vt = vs[pl.ds(krow0, K_WIN), :]
            sc = lax.dot_general(qt, kt, (((1,), (1,)), ((), ())),
                                 preferred_element_type=F32)
            sc = sc + mask_scr[which]
            m = jnp.max(sc, axis=-1, keepdims=True)
            e = jnp.exp2(sc - m).astype(BF16)
            ov = jnp.dot(e, vt, preferred_element_type=F32)
            num = ov[:, 0:HEAD_DIM]
            den = ov[:, HEAD_DIM:2 * HEAD_DIM]
            m = jnp.broadcast_to(m, (Q_TILE, HEAD_DIM))
            if p > 0:
                prev = states[p]
                m_prev = prev[1, rows, :]
                m_all = jnp.maximum(m_prev, m)
                w_prev = jnp.exp2(m_prev - m_all)
                w_cur = jnp.exp2(m - m_all)
                num = w_prev * prev[0, rows, :] + w_cur * num
                den = w_prev * prev[2, rows, :] + w_cur * den
                m = m_all
            if p + 1 < len(patterns):
                nxt, dil_next = states[p + 1], patterns[p + 1][0]
                dst0 = (seg % dil_next) * (s // dil_next) + (dil // dil_next) * l0 + seg // dil_next
                dst = pl.ds(dst0, Q_TILE, stride=dil // dil_next)
                nxt[0, dst, :] = num
                nxt[1, dst, :] = m
                nxt[2, dst, :] = den
            else:
                ya_ref[rows, :] = (num / den * za_ref[rows, :].astype(F32)).astype(BF16)

        def group_body(gi, carry, one_tile=one_tile):
            for u in range(TILE_GROUP):
                one_tile(gi * TILE_GROUP + u)
            return carry

        lax.fori_loop(0, s // (Q_TILE * TILE_GROUP), group_body, 0)


def _attention(proj):
    b, s, _ = proj.shape
    nblk = A_WIDTH // HEAD_DIM

    def col(slot):
        return pl.BlockSpec((None, s, HEAD_DIM), lambda bi, h: (bi, 0, slot * nblk + h))

    return pl.pallas_call(
        _attention_kernel,
        grid=(b, A_HEADS),
        in_specs=[col(0), col(1), col(2), col(3)],
        out_specs=pl.BlockSpec((None, s, HEAD_DIM), lambda bi, h: (bi, 0, h)),
        out_shape=jax.ShapeDtypeStruct((b, s, A_WIDTH), BF16),
        scratch_shapes=[
            pltpu.VMEM((3, s, HEAD_DIM), F32), pltpu.VMEM((3, s, HEAD_DIM), F32),
            pltpu.VMEM((s, HEAD_DIM), BF16), pltpu.VMEM((s, HEAD_DIM), BF16),
            pltpu.VMEM((s, 2 * HEAD_DIM), BF16),
            pltpu.VMEM((s, HEAD_DIM), BF16), pltpu.VMEM((s, HEAD_DIM), BF16),
            pltpu.VMEM((s, 2 * HEAD_DIM), BF16),
            pltpu.VMEM((s, 2 * HEAD_DIM), BF16),
            pltpu.VMEM((3, Q_TILE, K_WIN), F32),
        ],
        compiler_params=pltpu.CompilerParams(
            dimension_semantics=("arbitrary", "arbitrary"),
            vmem_limit_bytes=VMEM_LIMIT),
        name="attention",
    )(proj, proj, proj, proj)


def _ab_out_kernel(ya_ref, ub_ref, gb_ref, gc_ref, zb_ref, ubp_ref, gcp_ref, ubn_ref, gcn_ref,
                   cw_ref, w_ref, x_ref, gate_ref, ng_ref, nshift_ref, nscale_ref,
                   o_ref, *rest, final, row_groups):
    h_ref, pbuf, yb_scr = ((None,) + rest) if final else rest
    i = pl.program_id(1)
    tm = x_ref.shape[0]
    halo = ubp_ref.shape[0]
    chunk = 32

    prev = (gcp_ref[...].astype(F32) * ubp_ref[...].astype(F32))[halo - 1:halo, :]
    nxt = (gcn_ref[...].astype(F32) * ubn_ref[...].astype(F32))[0:1, :]
    pbuf[7:8, :] = jnp.where(i > 0, prev, 0.0)
    pbuf[8 + tm:9 + tm, :] = jnp.where(i < pl.num_programs(1) - 1, nxt, 0.0)
    for c0 in range(0, tm, chunk):
        rows = slice(c0, c0 + chunk)
        pbuf[8 + c0:8 + c0 + chunk, :] = gc_ref[rows, :].astype(F32) * ub_ref[rows, :].astype(F32)
    w0 = cw_ref[0:1, :]
    w1 = cw_ref[1:2, :]
    w2 = cw_ref[2:3, :]

    for d0, d1 in row_groups:
        for c0 in range(d0, d1, chunk):
            rows = slice(c0, c0 + chunk)
            conv = (w0 * pbuf[7 + c0:7 + c0 + chunk, :] + w1 * pbuf[8 + c0:8 + c0 + chunk, :]
                    + w2 * pbuf[9 + c0:9 + c0 + chunk, :])
            gate = gb_ref[rows, :].astype(F32) * zb_ref[rows, :].astype(F32)
            yb_scr[rows, :] = (gate * conv).astype(BF16)
        rows = slice(d0, d1)
        out = jnp.dot(ya_ref[rows, :], w_ref[0:A_WIDTH, :], preferred_element_type=F32)
        out = out + jnp.dot(yb_scr[rows, :], w_ref[A_WIDTH:A_WIDTH + B_WIDTH, :],
                            preferred_element_type=F32)
        xn = x_ref[rows, :] + gate_ref[...] * out
        _residual_tail(xn, rows, ng_ref, nshift_ref, nscale_ref, o_ref, h_ref, final)


def _residual_out(b, s, d, tm, final):
    spec = pl.BlockSpec((None, tm, d), lambda bi, i: (bi, i, 0))
    if final:
        return spec, jax.ShapeDtypeStruct((b, s, d), F32)
    return [spec, spec], [jax.ShapeDtypeStruct((b, s, d), F32),
                          jax.ShapeDtypeStruct((b, s, d), BF16)]


def _ab_out_projection(ya, proj, conv_w, w_bf16, layer, x, gate, next_g, next_shift, next_scale, *,
                       final, tm=512, halo=16):
    b, s, d = x.shape
    out_specs, out_shape = _residual_out(b, s, d, tm, final)
    row_vec = pl.BlockSpec((None, 1, d), lambda bi, i: (bi, 0, 0))
    per_tile = tm // halo
    n_halo = s // halo

    def slot(k):
        return pl.BlockSpec((None, tm, B_WIDTH), lambda bi, i: (bi, i, k))

    def before(k):
        return pl.BlockSpec((None, halo, B_WIDTH),
                            lambda bi, i: (bi, jnp.maximum(i * per_tile - 1, 0), k))

    def after(k):
        return pl.BlockSpec((None, halo, B_WIDTH),
                            lambda bi, i: (bi, jnp.minimum((i + 1) * per_tile, n_halo - 1), k))

    return pl.pallas_call(
        functools.partial(_ab_out_kernel, final=final, row_groups=_row_groups(tm)),
        grid=(b, s // tm),
        in_specs=[
            pl.BlockSpec((None, tm, A_WIDTH), lambda bi, i: (bi, i, 0)),
            slot(4), slot(5), slot(6), slot(7),
            before(4), before(6), after(4), after(6),
            pl.BlockSpec((B_CONV, B_WIDTH), lambda bi, i: (0, 0)),
            pl.BlockSpec((None, A_WIDTH + B_WIDTH, d), lambda bi, i: (layer, 0, 0),
                         pipeline_mode=pl.Buffered(1)),
            pl.BlockSpec((None, tm, d), lambda bi, i: (bi, i, 0)),
            row_vec,
            pl.BlockSpec((1, d), lambda bi, i: (0, 0)),
            row_vec,
            row_vec,
        ],
        out_specs=out_specs,
        out_shape=out_shape,
        scratch_shapes=[pltpu.VMEM((tm + 16, B_WIDTH), F32), pltpu.VMEM((tm, B_WIDTH), BF16)],
        compiler_params=pltpu.CompilerParams(
            dimension_semantics=("arbitrary", "arbitrary"),
            vmem_limit_bytes=VMEM_LIMIT),
        name="ab_out_projection",
    )(ya, proj, proj, proj, proj, proj, proj, proj, proj, conv_w, w_bf16, x, gate,
      next_g.reshape(1, d), next_shift, next_scale)


def _sgu_kernel(u_ref, v_ref, z_ref, lng_ref, lnb_ref, ws_ref, bs_ref, w_ref, x_ref, gate_ref,
                ng_ref, nshift_ref, nscale_ref, o_ref, *rest, final, row_groups):
    h_ref, y_scr = (None, rest[0]) if final else rest
    tm = u_ref.shape[0]
    gw = C_WIDTH // C_GROUPS
    lng = lng_ref[...]
    lnb = lnb_ref[...]
    for d0, d1 in row_groups:
        for c0 in range(d0, d1, C_CHUNK):
            rows = slice(c0, c0 + C_CHUNK)
            v = v_ref[rows, :].astype(F32)
            mu = jnp.mean(v, axis=-1, keepdims=True)
            vc = v - mu
            var = jnp.mean(vc * vc, axis=-1, keepdims=True)
            vn = (vc * lax.rsqrt(var + EPS) * lng + lnb).astype(BF16)
            for g in range(C_GROUPS):
                cols = slice(g * gw, (g + 1) * gw)
                mixed = jnp.dot(ws_ref[g], vn[:, cols], preferred_element_type=F32)
                mixed = mixed + bs_ref[:, g:g + 1]
                gated = u_ref[rows, cols].astype(F32) * z_ref[rows, cols].astype(F32)
                y_scr[rows, cols] = (gated * mixed).astype(BF16)
        rows = slice(d0, d1)
        out = jnp.dot(y_scr[rows, :], w_ref[...], preferred_element_type=F32)
        xn = x_ref[rows, :] + gate_ref[...] * out
        _residual_tail(xn, rows, ng_ref, nshift_ref, nscale_ref, o_ref, h_ref, final)


def _sgu(proj, ln_g, ln_b, ws_bf16, bs_t, w_bf16, layer, x, gate, next_g, next_shift, next_scale, *,
         final, tm=512):
    b, s, d = x.shape
    cw = C_WIDTH
    kern = functools.partial(_sgu_kernel, final=final, row_groups=_row_groups(tm))
    out_specs, out_shape = _residual_out(b, s, d, tm, final)
    row_vec = pl.BlockSpec((None, 1, d), lambda bi, i: (bi, 0, 0))
    return pl.pallas_call(
        kern,
        grid=(b, s // tm),
        in_specs=[
            pl.BlockSpec((None, tm, cw), lambda bi, i: (bi, i, 0)),
            pl.BlockSpec((None, tm, cw), lambda bi, i: (bi, i, 1)),
            pl.BlockSpec((None, tm, cw), lambda bi, i: (bi, i, 2)),
            pl.BlockSpec((1, cw), lambda bi, i: (0, 0)),
            pl.BlockSpec((1, cw), lambda bi, i: (0, 0)),
            pl.BlockSpec((None, C_GROUPS, C_CHUNK, C_CHUNK), lambda bi, i: (layer, 0, 0, 0)),
            pl.BlockSpec((C_CHUNK, C_GROUPS), lambda bi, i: (0, 0)),
            pl.BlockSpec((None, cw, d), lambda bi, i: (layer, 0, 0), pipeline_mode=pl.Buffered(1)),
            pl.BlockSpec((None, tm, d), lambda bi, i: (bi, i, 0)),
            row_vec,
            pl.BlockSpec((1, d), lambda bi, i: (0, 0)),
            row_vec,
            row_vec,
        ],
        out_specs=out_specs,
        out_shape=out_shape,
        scratch_shapes=[pltpu.VMEM((tm, cw), BF16)],
        compiler_params=pltpu.CompilerParams(
            dimension_semantics=("arbitrary", "arbitrary"),
            vmem_limit_bytes=VMEM_LIMIT),
        name="sgu_final" if final else "sgu",
    )(proj, proj, proj, ln_g.reshape(1, cw), ln_b.reshape(1, cw), ws_bf16, bs_t, w_bf16, x,
      gate, next_g.reshape(1, d), next_shift, next_scale)


def _rope_tables(s):
    half = HEAD_DIM // 2
    inv = ROPE_THETA ** (-jnp.arange(half, dtype=F32) / half)
    ang = jnp.arange(s, dtype=F32)[:, None] * inv[None, :]
    cos = jnp.cos(ang)
    sin = jnp.sin(ang)
    return (jnp.concatenate([cos, cos], axis=-1), jnp.concatenate([-sin, sin], axis=-1))


def _split_mod(mod, batch):
    m = mod[:batch].reshape(batch, 1, 3, D_MODEL)
    return m[:, :, 0, :], m[:, :, 1, :], m[:, :, 2, :]


def kernel(x, c, ab_norm_g, ab_w_mod, ab_b_mod, ab_w_in, ab_conv_w, ab_w_out, sg_norm_g, sg_w_mod, sg_b_mod, sg_w_in, sg_ln_g, sg_ln_b, sg_w_s, sg_b_s, sg_w_out, final_norm_g):
    batch, s, _ = x.shape
    depth = ab_w_in.shape[0] + sg_w_in.shape[0]
    c_pad = jnp.pad(c, ((0, 16 - batch), (0, 0)))
    mod_ab = _modulation(c_pad, ab_w_mod, ab_b_mod)
    mod_sg = _modulation(c_pad, sg_w_mod, sg_b_mod)
    cos, sin = _rope_tables(s)

    ab_w_in_b = ab_w_in.astype(BF16)
    ab_w_out_b = ab_w_out.astype(BF16)
    sg_w_in_b = sg_w_in.astype(BF16)
    sg_w_out_b = sg_w_out.astype(BF16)
    sg_w_s_b = sg_w_s.astype(BF16)

    params = []
    for layer in range(depth):
        i = layer // 2
        norm_g, mod = (ab_norm_g, mod_ab) if layer % 2 == 0 else (sg_norm_g, mod_sg)
        params.append((norm_g[i],) + _split_mod(mod[i], batch))

    h = None
    for layer in range(depth):
        i = layer // 2
        norm_g, shift, scale, gate = params[layer]
        final = layer == depth - 1
        nxt = (final_norm_g, gate, gate) if final else params[layer + 1][:3]
        src, normalize = (x, True) if h is None else (h, False)
        if layer % 2 == 0:
            proj = _in_projection(src, norm_g, shift, scale, ab_w_in_b, i, cos, sin,
                                  epilogues=AB_EPILOGUES, normalize=normalize)
            ya = _attention(proj)
            res = _ab_out_projection(ya, proj, ab_conv_w[i], ab_w_out_b, i, x, gate, *nxt,
                                     final=final)
        else:
            proj = _in_projection(src, norm_g, shift, scale, sg_w_in_b, i, cos, sin,
                                  epilogues=SG_EPILOGUES, normalize=normalize)
            res = _sgu(proj, sg_ln_g[i], sg_ln_b[i], sg_w_s_b, sg_b_s[i].T, sg_w_out_b, i, x, gate,
                       *nxt, final=final)
        if final:
            return res
        x, h = res
```

```python
import functools

import jax
import jax.numpy as jnp
from jax import lax
from jax.experimental import pallas as pl
from jax.experimental.pallas import tpu as pltpu

F32 = jnp.float32
BF16 = jnp.bfloat16

D_MODEL = 2048
HEAD_DIM = 128
A_WIDTH = 1024
A_HEADS = 8
B_WIDTH = 1024
B_CONV = 3
DILATIONS = (1, 4, 16)
RADIUS = 64
ROPE_THETA = 10000.0
NEG_INF = -1e30
C_WIDTH = 2048
C_GROUPS = 8
C_CHUNK = 128
EPS = 1e-6

Q_SCALE = HEAD_DIM ** -0.5 * 1.4426950408889634
Q_TILE = 128
K_WIN = Q_TILE + 2 * RADIUS
TILE_GROUP = 32
VMEM_LIMIT = 60 * 1024 * 1024

AB_EPILOGUES = (("rope_q", "rope_k"), ("none", "silu"), ("none", "none"), ("none", "silu"))
SG_EPILOGUES = (("gelu", "gelu"), ("gelu", "gelu"), ("silu", "silu"))


def _silu(z):
    hz = 0.5 * z
    return hz + hz * jnp.tanh(hz)


def _gelu_tanh(x):
    c = 0.7978845608028654
    return x * (0.5 * (1.0 + jnp.tanh(c * (x + 0.044715 * (x * x * x)))))


def _rms_norm(x, gain):
    ms = jnp.mean(x * x, axis=-1, keepdims=True)
    return x * lax.rsqrt(ms + EPS) * gain


def _modulated_rms_norm(x, gm, shift):
    return _rms_norm(x, gm) + shift


def _row_groups(tm):
    edges = (0, tm // 2, tm)
    return tuple(zip(edges[:-1], edges[1:]))


def _residual_tail(xn, rows, ng_ref, nshift_ref, nscale_ref, o_ref, h_ref, final):
    if final:
        o_ref[rows, :] = _rms_norm(xn, ng_ref[...])
    else:
        o_ref[rows, :] = xn
        gm = ng_ref[...] * (1.0 + nscale_ref[...])
        h_ref[rows, :] = _modulated_rms_norm(xn, gm, nshift_ref[...]).astype(BF16)


def _mod_kernel(c_ref, w_ref, b_ref, o_ref):
    a = _silu(c_ref[...]).astype(BF16)
    w = w_ref[...].astype(BF16)
    o_ref[...] = jnp.dot(a, w, preferred_element_type=F32) + b_ref[...]


def _modulation(c_pad, w_mod, b_mod, tn=1024):
    n_layers, d, n = w_mod.shape
    rows = c_pad.shape[0]
    return pl.pallas_call(
        _mod_kernel,
        grid=(n_layers, n // tn),
        in_specs=[
            pl.BlockSpec((rows, d), lambda l, j: (0, 0)),
            pl.BlockSpec((None, d, tn), lambda l, j: (l, 0, j)),
            pl.BlockSpec((None, 1, tn), lambda l, j: (l, 0, j)),
        ],
        out_specs=pl.BlockSpec((None, rows, tn), lambda l, j: (l, 0, j)),
        out_shape=jax.ShapeDtypeStruct((n_layers, rows, n), F32),
        compiler_params=pltpu.CompilerParams(
            dimension_semantics=("arbitrary", "arbitrary"),
            vmem_limit_bytes=VMEM_LIMIT),
        name="modulation",
    )(c_pad, w_mod, b_mod.reshape(n_layers, 1, n))


def _inproj_kernel(x_ref, g_ref, shift_ref, scale_ref, w_ref, cos_ref, sin_ref, o_ref,
                   *scratch, epilogues, seg, normalize, norm_groups, row_chunk, out_chunk):
    j = pl.program_id(2)
    tm = x_ref.shape[0]
    h_ref = scratch[0] if normalize else x_ref

    def normalize_rows(r0, r1):
        gm = g_ref[...] * (1.0 + scale_ref[...])
        shift = shift_ref[...]
        for c0 in range(r0, r1, row_chunk):
            rows = slice(c0, c0 + row_chunk)
            h_ref[rows, :] = _modulated_rms_norm(x_ref[rows, :], gm, shift).astype(BF16)

    def rope(acc, r0, c0, f):
        rows = slice(r0, r0 + acc.shape[0])
        cos = cos_ref[rows, :] * f
        sin = sin_ref[rows, :] * f
        for hh in range(seg // HEAD_DIM):
            cols = slice(c0 + hh * HEAD_DIM, c0 + (hh + 1) * HEAD_DIM)
            blk = acc[:, cols]
            o_ref[rows, cols] = (blk * cos + pltpu.roll(blk, HEAD_DIM // 2, 1) * sin).astype(BF16)

    def pointwise(acc, r0, c0, fn):
        for q0 in range(0, acc.shape[0], out_chunk):
            o_ref[r0 + q0:r0 + q0 + out_chunk, c0:c0 + seg] = fn(
                acc[q0:q0 + out_chunk, c0:c0 + seg]).astype(BF16)

    finish = {
        "rope_q": lambda acc, r0, c0: rope(acc, r0, c0, Q_SCALE),
        "rope_k": lambda acc, r0, c0: rope(acc, r0, c0, 1.0),
        "none": lambda acc, r0, c0: pointwise(acc, r0, c0, lambda t: t),
        "silu": lambda acc, r0, c0: pointwise(acc, r0, c0, _silu),
        "gelu": lambda acc, r0, c0: pointwise(acc, r0, c0, _gelu_tanh),
    }
    def project(kinds, groups, with_norm):
        for r0, r1 in groups:
            if with_norm:
                normalize_rows(r0, r1)
            acc = jnp.dot(h_ref[r0:r1, :], w_ref[...], preferred_element_type=F32)
            for si, kind in enumerate(kinds):
                finish[kind](acc, r0, si * seg)

    plain_tiles = list(enumerate(epilogues))
    if normalize:
        plain_tiles = plain_tiles[1:]

        @pl.when(j == 0)
        def _():
            project(epilogues[0], norm_groups, True)

    for kinds in sorted(set(e for _, e in plain_tiles)):
        tiles = [t for t, e in plain_tiles if e == kinds]
        cond = functools.reduce(jnp.logical_or, [j == t for t in tiles])

        @pl.when(cond)
        def _(kinds=kinds):
            project(kinds, ((0, tm),), False)


def _in_projection(x, norm_g, shift, scale, w_bf16, layer, cos, sin, *, epilogues, normalize,
                   tm=1024):
    b, s, d = x.shape
    n = w_bf16.shape[2]
    tn = n // len(epilogues)
    seg = tn // len(epilogues[0])
    kern = functools.partial(_inproj_kernel, epilogues=epilogues, seg=seg, normalize=normalize,
                             norm_groups=tuple((r, r + tm // 4) for r in range(0, tm, tm // 4)),
                             row_chunk=16, out_chunk=256)
    return pl.pallas_call(
        kern,
        grid=(b, s // tm, n // tn),
        in_specs=[
            pl.BlockSpec((None, tm, d), lambda bi, i, j: (bi, i, 0)),
            pl.BlockSpec((1, d), lambda bi, i, j: (0, 0)),
            pl.BlockSpec((None, 1, d), lambda bi, i, j: (bi, 0, 0)),
            pl.BlockSpec((None, 1, d), lambda bi, i, j: (bi, 0, 0)),
            pl.BlockSpec((None, d, tn), lambda bi, i, j: (layer, 0, j)),
            pl.BlockSpec((tm, HEAD_DIM), lambda bi, i, j: (i, 0)),
            pl.BlockSpec((tm, HEAD_DIM), lambda bi, i, j: (i, 0)),
        ],
        out_specs=pl.BlockSpec((None, tm, tn), lambda bi, i, j: (bi, i, j)),
        out_shape=jax.ShapeDtypeStruct((b, s, n), BF16),
        scratch_shapes=[pltpu.VMEM((tm, d), BF16)] if normalize else [],
        compiler_params=pltpu.CompilerParams(
            dimension_semantics=("arbitrary", "arbitrary", "arbitrary"),
            vmem_limit_bytes=VMEM_LIMIT),
        name="in_projection_ab" if "rope_q" in epilogues[0] else "in_projection_sg",
    )(x, norm_g.reshape(1, d), shift, scale, w_bf16, cos, sin)


def _attention_kernel(q_ref, k_ref, v_ref, za_ref, ya_ref,
                      fa, fb, q4, k4, v4, q16, k16, v16, v1, mask_scr):
    s = q_ref.shape[0]
    d4, d16 = DILATIONS[1], DILATIONS[2]
    n4, n16 = s // d4, s // d16
    step = d16 // d4
    chunk = 256

    qi = lax.broadcasted_iota(jnp.int32, (Q_TILE, K_WIN), 0)
    ki = lax.broadcasted_iota(jnp.int32, (Q_TILE, K_WIN), 1)
    for t in range(3):
        mask_scr[t] = jnp.where(jnp.abs(qi + t * RADIUS - ki) <= RADIUS, 0.0, NEG_INF)

    ones = jnp.ones((s, HEAD_DIM), BF16)
    v1[:, HEAD_DIM:2 * HEAD_DIM] = ones
    v4[:, HEAD_DIM:2 * HEAD_DIM] = ones
    v16[:, HEAD_DIM:2 * HEAD_DIM] = ones
    v1[:, 0:HEAD_DIM] = v_ref[...]

    for idx, (src, dst4, dst16) in enumerate(((q_ref, q4, q16), (k_ref, k4, k16),
                                              (v_ref, v4, v16))):
        for c0 in range(0, s, chunk):
            fa[idx, c0:c0 + chunk, :] = src[c0:c0 + chunk, :].astype(F32)
        for r in range(d4):
            for c0 in range(0, n4, chunk):
                part = fa[idx, pl.ds(r + d4 * c0, chunk, stride=d4), :]
                fb[idx, r * n4 + c0:r * n4 + c0 + chunk, :] = part
                dst4[r * n4 + c0:r * n4 + c0 + chunk, 0:HEAD_DIM] = part.astype(BF16)
        for r in range(d4):
            for a in range(step):
                part = fb[idx, pl.ds(r * n4 + a, n16, stride=step), :]
                r16 = r + d4 * a
                dst16[r16 * n16:(r16 + 1) * n16, 0:HEAD_DIM] = part.astype(BF16)

    patterns = ((d16, q16, k16, v16), (d4, q4, k4, v4), (1, q_ref, k_ref, v1))
    states = (None, fa, fb)
    for p, (dil, qs, ks, vs) in enumerate(patterns):
        n = s // dil
        tiles_per_seg = n // Q_TILE

        def one_tile(t, qs=qs, ks=ks, vs=vs, dil=dil, n=n, tiles_per_seg=tiles_per_seg, p=p):
            seg = t // tiles_per_seg
            l0 = (t % tiles_per_seg) * Q_TILE
            kstart = jnp.clip(l0 - RADIUS, 0, n - K_WIN)
            which = (l0 - kstart) // RADIUS
            row0 = pl.multiple_of(seg * n + l0, Q_TILE)
            krow0 = pl.multiple_of(seg * n + kstart, RADIUS)
            rows = pl.ds(row0, Q_TILE)
            qt = qs[rows, :]
            kt = ks[pl.ds(krow0, K_WIN), :]
            vt = vs[pl.ds(krow0, K_WIN), :]
            sc = lax.dot_general(qt, kt, (((1,), (1,)), ((), ())),
                                 preferred_element_type=F32)
            sc = sc + mask_scr[which]
            m = jnp.max(sc, axis=-1, keepdims=True)
            e = jnp.exp2(sc - m).astype(BF16)
            ov = jnp.dot(e, vt, preferred_element_type=F32)
            num = ov[:, 0:HEAD_DIM]
            den = ov[:, HEAD_DIM:2 * HEAD_DIM]
            m = jnp.broadcast_to(m, (Q_TILE, HEAD_DIM))
            if p > 0:
                prev = states[p]
                m_prev = prev[1, rows, :]
                m_all = jnp.maximum(m_prev, m)
                w_prev = jnp.exp2(m_prev - m_all)
                w_cur = jnp.exp2(m - m_all)
                num = w_prev * prev[0, rows, :] + w_cur * num
                den = w_prev * prev[2, rows, :] + w_cur * den
                m = m_all
            if p + 1 < len(patterns):
                nxt, dil_next = states[p + 1], patterns[p + 1][0]
                dst0 = (seg % dil_next) * (s // dil_next) + (dil // dil_next) * l0 + seg // dil_next
                dst = pl.ds(dst0, Q_TILE, stride=dil // dil_next)
                nxt[0, dst, :] = num
                nxt[1, dst, :] = m
                nxt[2, dst, :] = den
            else:
                ya_ref[rows, :] = (num / den * za_ref[rows, :].astype(F32)).astype(BF16)

        def group_body(gi, carry, one_tile=one_tile):
            for u in range(TILE_GROUP):
                one_tile(gi * TILE_GROUP + u)
            return carry

        lax.fori_loop(0, s // (Q_TILE * TILE_GROUP), group_body, 0)


def _attention(proj):
    b, s, _ = proj.shape
    nblk = A_WIDTH // HEAD_DIM

    def col(slot):
        return pl.BlockSpec((None, s, HEAD_DIM), lambda bi, h: (bi, 0, slot * nblk + h))

    return pl.pallas_call(
        _attention_kernel,
        grid=(b, A_HEADS),
        in_specs=[col(0), col(1), col(2), col(3)],
        out_specs=pl.BlockSpec((None, s, HEAD_DIM), lambda bi, h: (bi, 0, h)),
        out_shape=jax.ShapeDtypeStruct((b, s, A_WIDTH), BF16),
        scratch_shapes=[
            pltpu.VMEM((3, s, HEAD_DIM), F32), pltpu.VMEM((3, s, HEAD_DIM), F32),
            pltpu.VMEM((s, HEAD_DIM), BF16), pltpu.VMEM((s, HEAD_DIM), BF16),
            pltpu.VMEM((s, 2 * HEAD_DIM), BF16),
            pltpu.VMEM((s, HEAD_DIM), BF16), pltpu.VMEM((s, HEAD_DIM), BF16),
            pltpu.VMEM((s, 2 * HEAD_DIM), BF16),
            pltpu.VMEM((s, 2 * HEAD_DIM), BF16),
            pltpu.VMEM((3, Q_TILE, K_WIN), F32),
        ],
        compiler_params=pltpu.CompilerParams(
            dimension_semantics=("arbitrary", "arbitrary"),
            vmem_limit_bytes=VMEM_LIMIT),
        name="attention",
    )(proj, proj, proj, proj)


def _ab_out_kernel(ya_ref, ub_ref, gb_ref, gc_ref, zb_ref, ubp_ref, gcp_ref, ubn_ref, gcn_ref,
                   cw_ref, w_ref, x_ref, gate_ref, ng_ref, nshift_ref, nscale_ref,
                   o_ref, *rest, final, row_groups):
    h_ref, pbuf, yb_scr = ((None,) + rest) if final else rest
    i = pl.program_id(1)
    tm = x_ref.shape[0]
    halo = ubp_ref.shape[0]
    chunk = 32

    prev = (gcp_ref[...].astype(F32) * ubp_ref[...].astype(F32))[halo - 1:halo, :]
    nxt = (gcn_ref[...].astype(F32) * ubn_ref[...].astype(F32))[0:1, :]
    pbuf[7:8, :] = jnp.where(i > 0, prev, 0.0)
    pbuf[8 + tm:9 + tm, :] = jnp.where(i < pl.num_programs(1) - 1, nxt, 0.0)
    for c0 in range(0, tm, chunk):
        rows = slice(c0, c0 + chunk)
        pbuf[8 + c0:8 + c0 + chunk, :] = gc_ref[rows, :].astype(F32) * ub_ref[rows, :].astype(F32)
    w0 = cw_ref[0:1, :]
    w1 = cw_ref[1:2, :]
    w2 = cw_ref[2:3, :]

    for d0, d1 in row_groups:
        for c0 in range(d0, d1, chunk):
            rows = slice(c0, c0 + chunk)
            conv = (w0 * pbuf[7 + c0:7 + c0 + chunk, :] + w1 * pbuf[8 + c0:8 + c0 + chunk, :]
                    + w2 * pbuf[9 + c0:9 + c0 + chunk, :])
            gate = gb_ref[rows, :].astype(F32) * zb_ref[rows, :].astype(F32)
            yb_scr[rows, :] = (gate * conv).astype(BF16)
        rows = slice(d0, d1)
        out = jnp.dot(ya_ref[rows, :], w_ref[0:A_WIDTH, :], preferred_element_type=F32)
        out = out + jnp.dot(yb_scr[rows, :], w_ref[A_WIDTH:A_WIDTH + B_WIDTH, :],
                            preferred_element_type=F32)
        xn = x_ref[rows, :] + gate_ref[...] * out
        _residual_tail(xn, rows, ng_ref, nshift_ref, nscale_ref, o_ref, h_ref, final)


def _residual_out(b, s, d, tm, final):
    spec = pl.BlockSpec((None, tm, d), lambda bi, i: (bi, i, 0))
    if final:
        return spec, jax.ShapeDtypeStruct((b, s, d), F32)
    return [spec, spec], [jax.ShapeDtypeStruct((b, s, d), F32),
                          jax.ShapeDtypeStruct((b, s, d), BF16)]


def _ab_out_projection(ya, proj, conv_w, w_bf16, layer, x, gate, next_g, next_shift, next_scale, *,
                       final, tm=512, halo=16):
    b, s, d = x.shape
    out_specs, out_shape = _residual_out(b, s, d, tm, final)
    row_vec = pl.BlockSpec((None, 1, d), lambda bi, i: (bi, 0, 0))
    per_tile = tm // halo
    n_halo = s // halo

    def slot(k):
        return pl.BlockSpec((None, tm, B_WIDTH), lambda bi, i: (bi, i, k))

    def before(k):
        return pl.BlockSpec((None, halo, B_WIDTH),
                            lambda bi, i: (bi, jnp.maximum(i * per_tile - 1, 0), k))

    def after(k):
        return pl.BlockSpec((None, halo, B_WIDTH),
                            lambda bi, i: (bi, jnp.minimum((i + 1) * per_tile, n_halo - 1), k))

    return pl.pallas_call(
        functools.partial(_ab_out_kernel, final=final, row_groups=_row_groups(tm)),
        grid=(b, s // tm),
        in_specs=[
            pl.BlockSpec((None, tm, A_WIDTH), lambda bi, i: (bi, i, 0)),
            slot(4), slot(5), slot(6), slot(7),
            before(4), before(6), after(4), after(6),
            pl.BlockSpec((B_CONV, B_WIDTH), lambda bi, i: (0, 0)),
            pl.BlockSpec((None, A_WIDTH + B_WIDTH, d), lambda bi, i: (layer, 0, 0),
                         pipeline_mode=pl.Buffered(1)),
            pl.BlockSpec((None, tm, d), lambda bi, i: (bi, i, 0)),
            row_vec,
            pl.BlockSpec((1, d), lambda bi, i: (0, 0)),
            row_vec,
            row_vec,
        ],
        out_specs=out_specs,
        out_shape=out_shape,
        scratch_shapes=[pltpu.VMEM((tm + 16, B_WIDTH), F32), pltpu.VMEM((tm, B_WIDTH), BF16)],
        compiler_params=pltpu.CompilerParams(
            dimension_semantics=("arbitrary", "arbitrary"),
            vmem_limit_bytes=VMEM_LIMIT),
        name="ab_out_projection",
    )(ya, proj, proj, proj, proj, proj, proj, proj, proj, conv_w, w_bf16, x, gate,
      next_g.reshape(1, d), next_shift, next_scale)


def _sgu_kernel(u_ref, v_ref, z_ref, lng_ref, lnb_ref, ws_ref, bs_ref, w_ref, x_ref, gate_ref,
                ng_ref, nshift_ref, nscale_ref, o_ref, *rest, final, row_groups):
    h_ref, y_scr = (None, rest[0]) if final else rest
    tm = u_ref.shape[0]
    gw = C_WIDTH // C_GROUPS
    lng = lng_ref[...]
    lnb = lnb_ref[...]
    for d0, d1 in row_groups:
        for c0 in range(d0, d1, C_CHUNK):
            rows = slice(c0, c0 + C_CHUNK)
            v = v_ref[rows, :].astype(F32)
            mu = jnp.mean(v, axis=-1, keepdims=True)
            vc = v - mu
            var = jnp.mean(vc * vc, axis=-1, keepdims=True)
            vn = (vc * lax.rsqrt(var + EPS) * lng + lnb).astype(BF16)
            for g in range(C_GROUPS):
                cols = slice(g * gw, (g + 1) * gw)
                mixed = jnp.dot(ws_ref[g], vn[:, cols], preferred_element_type=F32)
                mixed = mixed + bs_ref[:, g:g + 1]
                gated = u_ref[rows, cols].astype(F32) * z_ref[rows, cols].astype(F32)
                y_scr[rows, cols] = (gated * mixed).astype(BF16)
        rows = slice(d0, d1)
        out = jnp.dot(y_scr[rows, :], w_ref[...], preferred_element_type=F32)
        xn = x_ref[rows, :] + gate_ref[...] * out
        _residual_tail(xn, rows, ng_ref, nshift_ref, nscale_ref, o_ref, h_ref, final)


def _sgu(proj, ln_g, ln_b, ws_bf16, bs_t, w_bf16, layer, x, gate, next_g, next_shift, next_scale, *,
         final, tm=512):
    b, s, d = x.shape
    cw = C_WIDTH
    kern = functools.partial(_sgu_kernel, final=final, row_groups=_row_groups(tm))
    out_specs, out_shape = _residual_out(b, s, d, tm, final)
    row_vec = pl.BlockSpec((None, 1, d), lambda bi, i: (bi, 0, 0))
    return pl.pallas_call(
        kern,
        grid=(b, s // tm),
        in_specs=[
            pl.BlockSpec((None, tm, cw), lambda bi, i: (bi, i, 0)),
            pl.BlockSpec((None, tm, cw), lambda bi, i: (bi, i, 1)),
            pl.BlockSpec((None, tm, cw), lambda bi, i: (bi, i, 2)),
            pl.BlockSpec((1, cw), lambda bi, i: (0, 0)),
            pl.BlockSpec((1, cw), lambda bi, i: (0, 0)),
            pl.BlockSpec((None, C_GROUPS, C_CHUNK, C_CHUNK), lambda bi, i: (layer, 0, 0, 0)),
            pl.BlockSpec((C_CHUNK, C_GROUPS), lambda bi, i: (0, 0)),
            pl.BlockSpec((None, cw, d), lambda bi, i: (layer, 0, 0), pipeline_mode=pl.Buffered(1)),
            pl.BlockSpec((None, tm, d), lambda bi, i: (bi, i, 0)),
            row_vec,
            pl.BlockSpec((1, d), lambda bi, i: (0, 0)),
            row_vec,
            row_vec,
        ],
        out_specs=out_specs,
        out_shape=out_shape,
        scratch_shapes=[pltpu.VMEM((tm, cw), BF16)],
        compiler_params=pltpu.CompilerParams(
            dimension_semantics=("arbitrary", "arbitrary"),
            vmem_limit_bytes=VMEM_LIMIT),
        name="sgu_final" if final else "sgu",
    )(proj, proj, proj, ln_g.reshape(1, cw), ln_b.reshape(1, cw), ws_bf16, bs_t, w_bf16, x,
      gate, next_g.reshape(1, d), next_shift, next_scale)


def _rope_tables(s):
    half = HEAD_DIM // 2
    inv = ROPE_THETA ** (-jnp.arange(half, dtype=F32) / half)
    ang = jnp.arange(s, dtype=F32)[:, None] * inv[None, :]
    cos = jnp.cos(ang)
    sin = jnp.sin(ang)
    return (jnp.concatenate([cos, cos], axis=-1), jnp.concatenate([-sin, sin], axis=-1))


def _split_mod(mod, batch):
    m = mod[:batch].reshape(batch, 1, 3, D_MODEL)
    return m[:, :, 0, :], m[:, :, 1, :], m[:, :, 2, :]


def kernel(x, c, ab_norm_g, ab_w_mod, ab_b_mod, ab_w_in, ab_conv_w, ab_w_out, sg_norm_g, sg_w_mod, sg_b_mod, sg_w_in, sg_ln_g, sg_ln_b, sg_w_s, sg_b_s, sg_w_out, final_norm_g):
    batch, s, _ = x.shape
    depth = ab_w_in.shape[0] + sg_w_in.shape[0]
    c_pad = jnp.pad(c, ((0, 16 - batch), (0, 0)))
    mod_ab = _modulation(c_pad, ab_w_mod, ab_b_mod)
    mod_sg = _modulation(c_pad, sg_w_mod, sg_b_mod)
    cos, sin = _rope_tables(s)

    ab_w_in_b = ab_w_in.astype(BF16)
    ab_w_out_b = ab_w_out.astype(BF16)
    sg_w_in_b = sg_w_in.astype(BF16)
    sg_w_out_b = sg_w_out.astype(BF16)
    sg_w_s_b = sg_w_s.astype(BF16)

    params = []
    for layer in range(depth):
        i = layer // 2
        norm_g, mod = (ab_norm_g, mod_ab) if layer % 2 == 0 else (sg_norm_g, mod_sg)
        params.append((norm_g[i],) + _split_mod(mod[i], batch))

    h = None
    for layer in range(depth):
        i = layer // 2
        norm_g, shift, scale, gate = params[layer]
        final = layer == depth - 1
        nxt = (final_norm_g, gate, gate) if final else params[layer + 1][:3]
        src, normalize = (x, True) if h is None else (h, False)
        if layer % 2 == 0:
            proj = _in_projection(src, norm_g, shift, scale, ab_w_in_b, i, cos, sin,
                                  epilogues=AB_EPILOGUES, normalize=normalize)
            ya = _attention(proj)
            res = _ab_out_projection(ya, proj, ab_conv_w[i], ab_w_out_b, i, x, gate, *nxt,
                                     final=final)
        else:
            proj = _in_projection(src, norm_g, shift, scale, sg_w_in_b, i, cos, sin,
                                  epilogues=SG_EPILOGUES, normalize=normalize)
            res = _sgu(proj, sg_ln_g[i], sg_ln_b[i], sg_w_s_b, sg_b_s[i].T, sg_w_out_b, i, x, gate,
                       *nxt, final=final)
        if final:
            return res
        x, h = res
```

```python
import functools

import jax
import jax.numpy as jnp
from jax import lax
from jax.experimental import pallas as pl
from jax.experimental.pallas import tpu as pltpu

F32 = jnp.float32
BF16 = jnp.bfloat16

D_MODEL = 2048
HEAD_DIM = 128
A_WIDTH = 1024
A_HEADS = 8
B_WIDTH = 1024
B_CONV = 3
DILATIONS = (1, 4, 16)
RADIUS = 64
ROPE_THETA = 10000.0
NEG_INF = -1e30
C_WIDTH = 2048
C_GROUPS = 8
C_CHUNK = 128
EPS = 1e-6

Q_SCALE = HEAD_DIM ** -0.5 * 1.4426950408889634
Q_TILE = 128
K_WIN = Q_TILE + 2 * RADIUS
TILE_GROUP = 32
CAST_SLABS = 32
VMEM_LIMIT = 60 * 1024 * 1024

AB_EPILOGUES = (("rope_q", "rope_k"), ("none", "silu"), ("none", "none"), ("none", "silu"))
SG_EPILOGUES = (("gelu", "gelu"), ("gelu", "gelu"), ("silu", "silu"))


def _silu(z):
    hz = 0.5 * z
    return hz + hz * jnp.tanh(hz)


def _gelu_tanh(x):
    c = 0.7978845608028654
    return x * (0.5 * (1.0 + jnp.tanh(c * (x + 0.044715 * (x * x * x)))))


def _rms_norm(x, gain):
    ms = jnp.mean(x * x, axis=-1, keepdims=True)
    return x * lax.rsqrt(ms + EPS) * gain


def _modulated_rms_norm(x, gm, shift):
    return _rms_norm(x, gm) + shift


def _row_groups(tm):
    edges = (0, tm // 2, tm)
    return tuple(zip(edges[:-1], edges[1:]))


def _residual_tail(xn, rows, ng_ref, nshift_ref, nscale_ref, o_ref, h_ref, final):
    if final:
        o_ref[rows, :] = _rms_norm(xn, ng_ref[...])
    else:
        o_ref[rows, :] = xn
        gm = ng_ref[...] * (1.0 + nscale_ref[...])
        h_ref[rows, :] = _modulated_rms_norm(xn, gm, nshift_ref[...]).astype(BF16)


def _mod_kernel(c_ref, w_ref, b_ref, o_ref):
    a = _silu(c_ref[...]).astype(BF16)
    w = w_ref[...].astype(BF16)
    o_ref[...] = jnp.dot(a, w, preferred_element_type=F32) + b_ref[...]


def _modulation(c_pad, w_mod, b_mod, tn=1024):
    n_layers, d, n = w_mod.shape
    rows = c_pad.shape[0]
    return pl.pallas_call(
        _mod_kernel,
        grid=(n_layers, n // tn),
        in_specs=[
            pl.BlockSpec((rows, d), lambda l, j: (0, 0)),
            pl.BlockSpec((None, d, tn), lambda l, j: (l, 0, j)),
            pl.BlockSpec((None, 1, tn), lambda l, j: (l, 0, j)),
        ],
        out_specs=pl.BlockSpec((None, rows, tn), lambda l, j: (l, 0, j)),
        out_shape=jax.ShapeDtypeStruct((n_layers, rows, n), F32),
        compiler_params=pltpu.CompilerParams(
            dimension_semantics=("arbitrary", "arbitrary"),
            vmem_limit_bytes=VMEM_LIMIT),
        name="modulation",
    )(c_pad, w_mod, b_mod.reshape(n_layers, 1, n))


def _inproj_kernel(x_ref, g_ref, shift_ref, scale_ref, w_ref, cos_ref, sin_ref, *rest,
                   n_casts, epilogues, seg, normalize, norm_groups, row_chunk, out_chunk):
    j = pl.program_id(2)
    tm = x_ref.shape[0]
    cast_src, o_ref, cast_dst = rest[:n_casts], rest[n_casts], rest[n_casts + 1:2 * n_casts + 1]
    scratch = rest[2 * n_casts + 1:]
    h_ref = scratch[0] if normalize else x_ref

    def normalize_rows(r0, r1):
        gm = g_ref[...] * (1.0 + scale_ref[...])
        shift = shift_ref[...]
        for c0 in range(r0, r1, row_chunk):
            rows = slice(c0, c0 + row_chunk)
            h_ref[rows, :] = _modulated_rms_norm(x_ref[rows, :], gm, shift).astype(BF16)

    def rope(acc, r0, c0, f):
        rows = slice(r0, r0 + acc.shape[0])
        cos = cos_ref[rows, :] * f
        sin = sin_ref[rows, :] * f
        for hh in range(seg // HEAD_DIM):
            cols = slice(c0 + hh * HEAD_DIM, c0 + (hh + 1) * HEAD_DIM)
            blk = acc[:, cols]
            o_ref[rows, cols] = (blk * cos + pltpu.roll(blk, HEAD_DIM // 2, 1) * sin).astype(BF16)

    def pointwise(acc, r0, c0, fn):
        for q0 in range(0, acc.shape[0], out_chunk):
            o_ref[r0 + q0:r0 + q0 + out_chunk, c0:c0 + seg] = fn(
                acc[q0:q0 + out_chunk, c0:c0 + seg]).astype(BF16)

    finish = {
        "rope_q": lambda acc, r0, c0: rope(acc, r0, c0, Q_SCALE),
        "rope_k": lambda acc, r0, c0: rope(acc, r0, c0, 1.0),
        "none": lambda acc, r0, c0: pointwise(acc, r0, c0, lambda t: t),
        "silu": lambda acc, r0, c0: pointwise(acc, r0, c0, _silu),
        "gelu": lambda acc, r0, c0: pointwise(acc, r0, c0, _gelu_tanh),
    }
    def project(kinds, groups, with_norm):
        for src, dst in zip(cast_src, cast_dst):
            dst[...] = src[...].astype(BF16)
        for r0, r1 in groups:
            if with_norm:
                normalize_rows(r0, r1)
            acc = jnp.dot(h_ref[r0:r1, :], w_ref[...], preferred_element_type=F32)
            for si, kind in enumerate(kinds):
                finish[kind](acc, r0, si * seg)

    plain_tiles = list(enumerate(epilogues))
    if normalize:
        plain_tiles = plain_tiles[1:]

        @pl.when(j == 0)
        def _():
            project(epilogues[0], norm_groups, True)

    for kinds in sorted(set(e for _, e in plain_tiles)):
        tiles = [t for t, e in plain_tiles if e == kinds]
        cond = functools.reduce(jnp.logical_or, [j == t for t in tiles])

        @pl.when(cond)
        def _(kinds=kinds):
            project(kinds, ((0, tm),), False)


def _in_projection(x, norm_g, shift, scale, w_bf16, cos, sin, *, epilogues, normalize, casts=(),
                   tm=1024):
    b, s, d = x.shape
    n = w_bf16.shape[1]
    tn = n // len(epilogues)
    seg = tn // len(epilogues[0])
    n_i, n_j = s // tm, n // tn
    assert b * n_i * n_j >= CAST_SLABS

    def slab(bi, i, j):
        return jnp.minimum((bi * n_i + i) * n_j + j, CAST_SLABS - 1)

    cast_in, cast_out, cast_shape = [], [], []
    for stack, layer in casts:
        _, rows, cols = stack.shape
        cast_in.append(pl.BlockSpec((None, rows // CAST_SLABS, cols),
                                    lambda bi, i, j, layer=layer: (layer, slab(bi, i, j), 0)))
        cast_out.append(pl.BlockSpec((rows // CAST_SLABS, cols),
                                     lambda bi, i, j: (slab(bi, i, j), 0)))
        cast_shape.append(jax.ShapeDtypeStruct((rows, cols), BF16))
    kern = functools.partial(_inproj_kernel, n_casts=len(casts), epilogues=epilogues, seg=seg,
                             normalize=normalize,
                             norm_groups=tuple((r, r + tm // 4) for r in range(0, tm, tm // 4)),
                             row_chunk=16, out_chunk=256)
    return pl.pallas_call(
        kern,
        grid=(b, s // tm, n // tn),
        in_specs=[
            pl.BlockSpec((None, tm, d), lambda bi, i, j: (bi, i, 0)),
            pl.BlockSpec((1, d), lambda bi, i, j: (0, 0)),
            pl.BlockSpec((None, 1, d), lambda bi, i, j: (bi, 0, 0)),
            pl.BlockSpec((None, 1, d), lambda bi, i, j: (bi, 0, 0)),
            pl.BlockSpec((d, tn), lambda bi, i, j: (0, j)),
            pl.BlockSpec((tm, HEAD_DIM), lambda bi, i, j: (i, 0)),
            pl.BlockSpec((tm, HEAD_DIM), lambda bi, i, j: (i, 0)),
        ] + cast_in,
        out_specs=[pl.BlockSpec((None, tm, tn), lambda bi, i, j: (bi, i, j))] + cast_out,
        out_shape=[jax.ShapeDtypeStruct((b, s, n), BF16)] + cast_shape,
        scratch_shapes=[pltpu.VMEM((tm, d), BF16)] if normalize else [],
        compiler_params=pltpu.CompilerParams(
            dimension_semantics=("arbitrary", "arbitrary", "arbitrary"),
            vmem_limit_bytes=VMEM_LIMIT),
        name="in_projection_ab" if "rope_q" in epilogues[0] else "in_projection_sg",
    )(x, norm_g.reshape(1, d), shift, scale, w_bf16, cos, sin, *[stack for stack, _ in casts])


def _attention_kernel(q_ref, k_ref, v_ref, za_ref, ya_ref,
                      fa, fb, q4, k4, v4, q16, k16, v16, v1, mask_scr):
    s = q_ref.shape[0]
    d4, d16 = DILATIONS[1], DILATIONS[2]
    n4, n16 = s // d4, s // d16
    step = d16 // d4
    chunk = 256

    qi = lax.broadcasted_iota(jnp.int32, (Q_TILE, K_WIN), 0)
    ki = lax.broadcasted_iota(jnp.int32, (Q_TILE, K_WIN), 1)
    for t in range(3):
        mask_scr[t] = jnp.where(jnp.abs(qi + t * RADIUS - ki) <= RADIUS, 0.0, NEG_INF)

    ones = jnp.ones((s, HEAD_DIM), BF16)
    v1[:, HEAD_DIM:2 * HEAD_DIM] = ones
    v4[:, HEAD_DIM:2 * HEAD_DIM] = ones
    v16[:, HEAD_DIM:2 * HEAD_DIM] = ones
    v1[:, 0:HEAD_DIM] = v_ref[...]

    for idx, (src, dst4, dst16) in enumerate(((q_ref, q4, q16), (k_ref, k4, k16),
                                              (v_ref, v4, v16))):
        for c0 in range(0, s, chunk):
            fa[idx, c0:c0 + chunk, :] = src[c0:c0 + chunk, :].astype(F32)
        for r in range(d4):
            for c0 in range(0, n4, chunk):
                part = fa[idx, pl.ds(r + d4 * c0, chunk, stride=d4), :]
                fb[idx, r * n4 + c0:r * n4 + c0 + chunk, :] = part
                dst4[r * n4 + c0:r * n4 + c0 + chunk, 0:HEAD_DIM] = part.astype(BF16)
        for r in range(d4):
            for a in range(step):
                part = fb[idx, pl.ds(r * n4 + a, n16, stride=step), :]
                r16 = r + d4 * a
                dst16[r16 * n16:(r16 + 1) * n16, 0:HEAD_DIM] = part.astype(BF16)

    patterns = ((d16, q16, k16, v16), (d4, q4, k4, v4), (1, q_ref, k_ref, v1))
    states = (None, fa, fb)
    for p, (dil, qs, ks, vs) in enumerate(patterns):
        n = s // dil
        tiles_per_seg = n // Q_TILE

        def one_tile(t, qs=qs, ks=ks, vs=vs, dil=dil, n=n, tiles_per_seg=tiles_per_seg, p=p):
            seg = t // tiles_per_seg
            l0 = (t % tiles_per_seg) * Q_TILE
            kstart = jnp.clip(l0 - RADIUS, 0, n - K_WIN)
            which = (l0 - kstart) // RADIUS
            row0 = pl.multiple_of(seg * n + l0, Q_TILE)
            krow0 = pl.multiple_of(seg * n + kstart, RADIUS)
            rows = pl.ds(row0, Q_TILE)
            qt = qs[rows, :]
            kt = ks[pl.ds(krow0, K_WIN), :]
            vt = vs[pl.ds(krow0, K_WIN), :]
            sc = lax.dot_general(qt, kt, (((1,), (1,)), ((), ())),
                                 preferred_element_type=F32)
            sc = sc + mask_scr[which]
            m = jnp.max(sc, axis=-1, keepdims=True)
            e = jnp.exp2(sc - m).astype(BF16)
            ov = jnp.dot(e, vt, preferred_element_type=F32)
            num = ov[:, 0:HEAD_DIM]
            den = ov[:, HEAD_DIM:2 * HEAD_DIM]
            m = jnp.broadcast_to(m, (Q_TILE, HEAD_DIM))
            if p > 0:
                prev = states[p]
                m_prev = prev[1, rows, :]
                m_all = jnp.maximum(m_prev, m)
                w_prev = jnp.exp2(m_prev - m_all)
                w_cur = jnp.exp2(m - m_all)
                num = w_prev * prev[0, rows, :] + w_cur * num
                den = w_prev * prev[2, rows, :] + w_cur * den
                m = m_all
            if p + 1 < len(patterns):
                nxt, dil_next = states[p + 1], patterns[p + 1][0]
                dst0 = (seg % dil_next) * (s // dil_next) + (dil // dil_next) * l0 + seg // dil_next
                dst = pl.ds(dst0, Q_TILE, stride=dil // dil_next)
                nxt[0, dst, :] = num
                nxt[1, dst, :] = m
                nxt[2, dst, :] = den
            else:
                ya_ref[rows, :] = (num / den * za_ref[rows, :].astype(F32)).astype(BF16)

        def group_body(gi, carry, one_tile=one_tile):
            for u in range(TILE_GROUP):
                one_tile(gi * TILE_GROUP + u)
            return carry

        lax.fori_loop(0, s // (Q_TILE * TILE_GROUP), group_body, 0)


def _attention(proj):
    b, s, _ = proj.shape
    nblk = A_WIDTH // HEAD_DIM

    def col(slot):
        return pl.BlockSpec((None, s, HEAD_DIM), lambda bi, h: (bi, 0, slot * nblk + h))

    return pl.pallas_call(
        _attention_kernel,
        grid=(b, A_HEADS),
        in_specs=[col(0), col(1), col(2), col(3)],
        out_specs=pl.BlockSpec((None, s, HEAD_DIM), lambda bi, h: (bi, 0, h)),
        out_shape=jax.ShapeDtypeStruct((b, s, A_WIDTH), BF16),
        scratch_shapes=[
            pltpu.VMEM((3, s, HEAD_DIM), F32), pltpu.VMEM((3, s, HEAD_DIM), F32),
            pltpu.VMEM((s, HEAD_DIM), BF16), pltpu.VMEM((s, HEAD_DIM), BF16),
            pltpu.VMEM((s, 2 * HEAD_DIM), BF16),
            pltpu.VMEM((s, HEAD_DIM), BF16), pltpu.VMEM((s, HEAD_DIM), BF16),
            pltpu.VMEM((s, 2 * HEAD_DIM), BF16),
            pltpu.VMEM((s, 2 * HEAD_DIM), BF16),
            pltpu.VMEM((3, Q_TILE, K_WIN), F32),
        ],
        compiler_params=pltpu.CompilerParams(
            dimension_semantics=("arbitrary", "arbitrary"),
            vmem_limit_bytes=VMEM_LIMIT),
        name="attention",
    )(proj, proj, proj, proj)


def _ab_out_kernel(ya_ref, ub_ref, gb_ref, gc_ref, zb_ref, ubp_ref, gcp_ref, ubn_ref, gcn_ref,
                   cw_ref, w_ref, x_ref, gate_ref, ng_ref, nshift_ref, nscale_ref,
                   o_ref, *rest, final, row_groups):
    h_ref, pbuf, yb_scr = ((None,) + rest) if final else rest
    i = pl.program_id(1)
    tm = x_ref.shape[0]
    halo = ubp_ref.shape[0]
    chunk = 32

    prev = (gcp_ref[...].astype(F32) * ubp_ref[...].astype(F32))[halo - 1:halo, :]
    nxt = (gcn_ref[...].astype(F32) * ubn_ref[...].astype(F32))[0:1, :]
    pbuf[7:8, :] = jnp.where(i > 0, prev, 0.0)
    pbuf[8 + tm:9 + tm, :] = jnp.where(i < pl.num_programs(1) - 1, nxt, 0.0)
    for c0 in range(0, tm, chunk):
        rows = slice(c0, c0 + chunk)
        pbuf[8 + c0:8 + c0 + chunk, :] = gc_ref[rows, :].astype(F32) * ub_ref[rows, :].astype(F32)
    w0 = cw_ref[0:1, :]
    w1 = cw_ref[1:2, :]
    w2 = cw_ref[2:3, :]

    for d0, d1 in row_groups:
        for c0 in range(d0, d1, chunk):
            rows = slice(c0, c0 + chunk)
            conv = (w0 * pbuf[7 + c0:7 + c0 + chunk, :] + w1 * pbuf[8 + c0:8 + c0 + chunk, :]
                    + w2 * pbuf[9 + c0:9 + c0 + chunk, :])
            gate = gb_ref[rows, :].astype(F32) * zb_ref[rows, :].astype(F32)
            yb_scr[rows, :] = (gate * conv).astype(BF16)
        rows = slice(d0, d1)
        out = jnp.dot(ya_ref[rows, :], w_ref[0:A_WIDTH, :], preferred_element_type=F32)
        out = out + jnp.dot(yb_scr[rows, :], w_ref[A_WIDTH:A_WIDTH + B_WIDTH, :],
                            preferred_element_type=F32)
        xn = x_ref[rows, :] + gate_ref[...] * out
        _residual_tail(xn, rows, ng_ref, nshift_ref, nscale_ref, o_ref, h_ref, final)


def _residual_out(b, s, d, tm, final):
    spec = pl.BlockSpec((None, tm, d), lambda bi, i: (bi, i, 0))
    if final:
        return spec, jax.ShapeDtypeStruct((b, s, d), F32)
    return [spec, spec], [jax.ShapeDtypeStruct((b, s, d), F32),
                          jax.ShapeDtypeStruct((b, s, d), BF16)]


def _ab_out_projection(ya, proj, conv_w, w_bf16, x, gate, next_g, next_shift, next_scale, *,
                       final, tm=512, halo=16):
    b, s, d = x.shape
    out_specs, out_shape = _residual_out(b, s, d, tm, final)
    row_vec = pl.BlockSpec((None, 1, d), lambda bi, i: (bi, 0, 0))
    per_tile = tm // halo
    n_halo = s // halo

    def slot(k):
        return pl.BlockSpec((None, tm, B_WIDTH), lambda bi, i: (bi, i, k))

    def before(k):
        return pl.BlockSpec((None, halo, B_WIDTH),
                            lambda bi, i: (bi, jnp.maximum(i * per_tile - 1, 0), k))

    def after(k):
        return pl.BlockSpec((None, halo, B_WIDTH),
                            lambda bi, i: (bi, jnp.minimum((i + 1) * per_tile, n_halo - 1), k))

    return pl.pallas_call(
        functools.partial(_ab_out_kernel, final=final, row_groups=_row_groups(tm)),
        grid=(b, s // tm),
        in_specs=[
            pl.BlockSpec((None, tm, A_WIDTH), lambda bi, i: (bi, i, 0)),
            slot(4), slot(5), slot(6), slot(7),
            before(4), before(6), after(4), after(6),
            pl.BlockSpec((B_CONV, B_WIDTH), lambda bi, i: (0, 0)),
            pl.BlockSpec((A_WIDTH + B_WIDTH, d), lambda bi, i: (0, 0),
                         pipeline_mode=pl.Buffered(1)),
            pl.BlockSpec((None, tm, d), lambda bi, i: (bi, i, 0)),
            row_vec,
            pl.BlockSpec((1, d), lambda bi, i: (0, 0)),
            row_vec,
            row_vec,
        ],
        out_specs=out_specs,
        out_shape=out_shape,
        scratch_shapes=[pltpu.VMEM((tm + 16, B_WIDTH), F32), pltpu.VMEM((tm, B_WIDTH), BF16)],
        compiler_params=pltpu.CompilerParams(
            dimension_semantics=("arbitrary", "arbitrary"),
            vmem_limit_bytes=VMEM_LIMIT),
        name="ab_out_projection",
    )(ya, proj, proj, proj, proj, proj, proj, proj, proj, conv_w, w_bf16, x, gate,
      next_g.reshape(1, d), next_shift, next_scale)


def _sgu_kernel(u_ref, v_ref, z_ref, lng_ref, lnb_ref, ws_ref, bs_ref, w_ref, x_ref, gate_ref,
                ng_ref, nshift_ref, nscale_ref, o_ref, *rest, final, row_groups):
    h_ref, y_scr = (None, rest[0]) if final else rest
    tm = u_ref.shape[0]
    gw = C_WIDTH // C_GROUPS
    lng = lng_ref[...]
    lnb = lnb_ref[...]
    for d0, d1 in row_groups:
        for c0 in range(d0, d1, C_CHUNK):
            rows = slice(c0, c0 + C_CHUNK)
            v = v_ref[rows, :].astype(F32)
            mu = jnp.mean(v, axis=-1, keepdims=True)
            vc = v - mu
            var = jnp.mean(vc * vc, axis=-1, keepdims=True)
            vn = (vc * lax.rsqrt(var + EPS) * lng + lnb).astype(BF16)
            for g in range(C_GROUPS):
                cols = slice(g * gw, (g + 1) * gw)
                mixed = jnp.dot(ws_ref[g], vn[:, cols], preferred_element_type=F32)
                mixed = mixed + bs_ref[:, g:g + 1]
                gated = u_ref[rows, cols].astype(F32) * z_ref[rows, cols].astype(F32)
                y_scr[rows, cols] = (gated * mixed).astype(BF16)
        rows = slice(d0, d1)
        out = jnp.dot(y_scr[rows, :], w_ref[...], preferred_element_type=F32)
        xn = x_ref[rows, :] + gate_ref[...] * out
        _residual_tail(xn, rows, ng_ref, nshift_ref, nscale_ref, o_ref, h_ref, final)


def _sgu(proj, ln_g, ln_b, ws_bf16, bs_t, w_bf16, x, gate, next_g, next_shift, next_scale, *,
         final, tm=512):
    b, s, d = x.shape
    cw = C_WIDTH
    kern = functools.partial(_sgu_kernel, final=final, row_groups=_row_groups(tm))
    out_specs, out_shape = _residual_out(b, s, d, tm, final)
    row_vec = pl.BlockSpec((None, 1, d), lambda bi, i: (bi, 0, 0))
    return pl.pallas_call(
        kern,
        grid=(b, s // tm),
        in_specs=[
            pl.BlockSpec((None, tm, cw), lambda bi, i: (bi, i, 0)),
            pl.BlockSpec((None, tm, cw), lambda bi, i: (bi, i, 1)),
            pl.BlockSpec((None, tm, cw), lambda bi, i: (bi, i, 2)),
            pl.BlockSpec((1, cw), lambda bi, i: (0, 0)),
            pl.BlockSpec((1, cw), lambda bi, i: (0, 0)),
            pl.BlockSpec((C_GROUPS, C_CHUNK, C_CHUNK), lambda bi, i: (0, 0, 0)),
            pl.BlockSpec((C_CHUNK, C_GROUPS), lambda bi, i: (0, 0)),
            pl.BlockSpec((cw, d), lambda bi, i: (0, 0), pipeline_mode=pl.Buffered(1)),
            pl.BlockSpec((None, tm, d), lambda bi, i: (bi, i, 0)),
            row_vec,
            pl.BlockSpec((1, d), lambda bi, i: (0, 0)),
            row_vec,
            row_vec,
        ],
        out_specs=out_specs,
        out_shape=out_shape,
        scratch_shapes=[pltpu.VMEM((tm, cw), BF16)],
        compiler_params=pltpu.CompilerParams(
            dimension_semantics=("arbitrary", "arbitrary"),
            vmem_limit_bytes=VMEM_LIMIT),
        name="sgu_final" if final else "sgu",
    )(proj, proj, proj, ln_g.reshape(1, cw), ln_b.reshape(1, cw), ws_bf16, bs_t, w_bf16, x,
      gate, next_g.reshape(1, d), next_shift, next_scale)


def _rope_tables(s):
    half = HEAD_DIM // 2
    inv = ROPE_THETA ** (-jnp.arange(half, dtype=F32) / half)
    ang = jnp.arange(s, dtype=F32)[:, None] * inv[None, :]
    cos = jnp.cos(ang)
    sin = jnp.sin(ang)
    return (jnp.concatenate([cos, cos], axis=-1), jnp.concatenate([-sin, sin], axis=-1))


def _split_mod(mod, batch):
    m = mod[:batch].reshape(batch, 1, 3, D_MODEL)
    return m[:, :, 0, :], m[:, :, 1, :], m[:, :, 2, :]


def kernel(x, c, ab_norm_g, ab_w_mod, ab_b_mod, ab_w_in, ab_conv_w, ab_w_out, sg_norm_g, sg_w_mod, sg_b_mod, sg_w_in, sg_ln_g, sg_ln_b, sg_w_s, sg_b_s, sg_w_out, final_norm_g):
    batch, s, _ = x.shape
    depth = ab_w_in.shape[0] + sg_w_in.shape[0]
    c_pad = jnp.pad(c, ((0, 16 - batch), (0, 0)))
    mod_ab = _modulation(c_pad, ab_w_mod, ab_b_mod)
    mod_sg = _modulation(c_pad, sg_w_mod, sg_b_mod)
    cos, sin = _rope_tables(s)

    w_in_b = ab_w_in[0].astype(BF16)
    sg_w_s_b = sg_w_s.astype(BF16)

    params = []
    for layer in range(depth):
        i = layer // 2
        norm_g, mod = (ab_norm_g, mod_ab) if layer % 2 == 0 else (sg_norm_g, mod_sg)
        params.append((norm_g[i],) + _split_mod(mod[i], batch))

    h = None
    for layer in range(depth):
        i = layer // 2
        norm_g, shift, scale, gate = params[layer]
        final = layer == depth - 1
        nxt = (final_norm_g, gate, gate) if final else params[layer + 1][:3]
        src, normalize = (x, True) if h is None else (h, False)
        if layer % 2 == 0:
            casts = ((ab_w_out, i),) + (() if final else ((sg_w_in, i),))
            proj, w_out_b, *w_next = _in_projection(src, norm_g, shift, scale, w_in_b, cos, sin,
                                                    epilogues=AB_EPILOGUES, normalize=normalize,
                                                    casts=casts)
            ya = _attention(proj)
            res = _ab_out_projection(ya, proj, ab_conv_w[i], w_out_b, x, gate, *nxt, final=final)
        else:
            casts = ((sg_w_out, i),) + (() if final else ((ab_w_in, i + 1),))
            proj, w_out_b, *w_next = _in_projection(src, norm_g, shift, scale, w_in_b, cos, sin,
                                                    epilogues=SG_EPILOGUES, normalize=normalize,
                                                    casts=casts)
            res = _sgu(proj, sg_ln_g[i], sg_ln_b[i], sg_w_s_b[i], sg_b_s[i].T, w_out_b, x, gate,
                       *nxt, final=final)
        if final:
            return res
        x, h = res
        w_in_b, = w_next
```

```python
import functools

import jax
import jax.numpy as jnp
from jax import lax
from jax.experimental import pallas as pl
from jax.experimental.pallas import tpu as pltpu

F32 = jnp.float32
BF16 = jnp.bfloat16

D_MODEL = 2048
HEAD_DIM = 128
A_WIDTH = 1024
A_HEADS = 8
B_WIDTH = 1024
B_CONV = 3
DILATIONS = (1, 4, 16)
RADIUS = 64
ROPE_THETA = 10000.0
NEG_INF = -1e30
C_WIDTH = 2048
C_GROUPS = 8
C_CHUNK = 128
EPS = 1e-6

LOG2_E = 1.4426950408889634
Q_SCALE = HEAD_DIM ** -0.5 * LOG2_E
Q_TILE = 128
K_WIN = Q_TILE + 2 * RADIUS
TILE_GROUP = 32
CAST_SLABS = 32
BF16_ROWS = 16
VMEM_LIMIT = 60 * 1024 * 1024

AB_EPILOGUES = (("rope_q", "rope_k"), ("none", "silu"), ("none", "none"), ("none", "silu"))
SG_EPILOGUES = (("gelu", "gelu"), ("gelu", "gelu"), ("silu", "silu"))


def _silu(z):
    hz = 0.5 * z
    return hz + hz * jnp.tanh(hz)


def _gelu_tanh(x):
    c = 0.7978845608028654
    hx = 0.5 * x
    return hx + hx * jnp.tanh(x * (c + (c * 0.044715) * (x * x)))


def _rms_norm(x, gain):
    ms = jnp.mean(x * x, axis=-1, keepdims=True)
    return x * lax.rsqrt(ms + EPS) * gain


def _modulated_rms_norm(x, gm, shift):
    return _rms_norm(x, gm) + shift


def _row_groups(tm):
    edges = (0, tm // 2, tm)
    return tuple(zip(edges[:-1], edges[1:]))


def _residual_tail(xn, rows, ng_ref, nshift_ref, nscale_ref, o_ref, h_ref, final):
    if final:
        o_ref[rows, :] = _rms_norm(xn, ng_ref[...])
    else:
        o_ref[rows, :] = xn
        gm = ng_ref[...] * (1.0 + nscale_ref[...])
        h_ref[rows, :] = _modulated_rms_norm(xn, gm, nshift_ref[...]).astype(BF16)


def _mod_kernel(c_ref, w_ref, b_ref, o_ref):
    a = _silu(c_ref[...]).astype(BF16)
    w = w_ref[...].astype(BF16)
    o_ref[...] = jnp.dot(a, w, preferred_element_type=F32) + b_ref[...]


def _modulation(c_pad, w_mod, b_mod, tn=1024):
    n_layers, d, n = w_mod.shape
    rows = c_pad.shape[0]
    return pl.pallas_call(
        _mod_kernel,
        grid=(n_layers, n // tn),
        in_specs=[
            pl.BlockSpec((rows, d), lambda l, j: (0, 0)),
            pl.BlockSpec((None, d, tn), lambda l, j: (l, 0, j)),
            pl.BlockSpec((None, 1, tn), lambda l, j: (l, 0, j)),
        ],
        out_specs=pl.BlockSpec((None, rows, tn), lambda l, j: (l, 0, j)),
        out_shape=jax.ShapeDtypeStruct((n_layers, rows, n), F32),
        compiler_params=pltpu.CompilerParams(
            dimension_semantics=("arbitrary", "arbitrary"),
            vmem_limit_bytes=VMEM_LIMIT),
        name="modulation",
    )(c_pad, w_mod, b_mod.reshape(n_layers, 1, n))


def _inproj_kernel(x_ref, g_ref, shift_ref, scale_ref, w_ref, cos_ref, sin_ref, *rest,
                   n_casts, epilogues, seg, normalize, norm_groups, row_chunk, out_chunk):
    j = pl.program_id(2)
    tm = x_ref.shape[0]
    cast_src, o_ref, cast_dst = rest[:n_casts], rest[n_casts], rest[n_casts + 1:2 * n_casts + 1]
    scratch = rest[2 * n_casts + 1:]
    h_ref = scratch[0] if normalize else x_ref

    def normalize_rows(r0, r1):
        gm = g_ref[...] * (1.0 + scale_ref[...])
        shift = shift_ref[...]
        for c0 in range(r0, r1, row_chunk):
            rows = slice(c0, c0 + row_chunk)
            h_ref[rows, :] = _modulated_rms_norm(x_ref[rows, :], gm, shift).astype(BF16)

    def rope(acc, r0, c0, f):
        rows = slice(r0, r0 + acc.shape[0])
        cos = cos_ref[rows, :] * f
        sin = sin_ref[rows, :] * f
        for hh in range(seg // HEAD_DIM):
            cols = slice(c0 + hh * HEAD_DIM, c0 + (hh + 1) * HEAD_DIM)
            blk = acc[:, cols]
            o_ref[rows, cols] = (blk * cos + pltpu.roll(blk, HEAD_DIM // 2, 1) * sin).astype(BF16)

    def pointwise(acc, r0, c0, fn):
        for q0 in range(0, acc.shape[0], out_chunk):
            o_ref[r0 + q0:r0 + q0 + out_chunk, c0:c0 + seg] = fn(
                acc[q0:q0 + out_chunk, c0:c0 + seg]).astype(BF16)

    finish = {
        "rope_q": lambda acc, r0, c0: rope(acc, r0, c0, Q_SCALE),
        "rope_k": lambda acc, r0, c0: rope(acc, r0, c0, 1.0),
        "none": lambda acc, r0, c0: pointwise(acc, r0, c0, lambda t: t),
        "silu": lambda acc, r0, c0: pointwise(acc, r0, c0, _silu),
        "gelu": lambda acc, r0, c0: pointwise(acc, r0, c0, _gelu_tanh),
    }
    def project(kinds, groups, with_norm):
        for src, dst in zip(cast_src, cast_dst):
            dst[...] = src[...].astype(BF16)
        for r0, r1 in groups:
            if with_norm:
                normalize_rows(r0, r1)
            acc = jnp.dot(h_ref[r0:r1, :], w_ref[...], preferred_element_type=F32)
            for si, kind in enumerate(kinds):
                finish[kind](acc, r0, si * seg)

    plain_tiles = list(enumerate(epilogues))
    if normalize:
        plain_tiles = plain_tiles[1:]

        @pl.when(j == 0)
        def _():
            project(epilogues[0], norm_groups, True)

    for kinds in sorted(set(e for _, e in plain_tiles)):
        tiles = [t for t, e in plain_tiles if e == kinds]
        cond = functools.reduce(jnp.logical_or, [j == t for t in tiles])

        @pl.when(cond)
        def _(kinds=kinds):
            project(kinds, ((0, tm),), False)


def _in_projection(x, norm_g, shift, scale, w_bf16, cos, sin, *, epilogues, normalize, casts=(),
                   tm=1024):
    b, s, d = x.shape
    n = w_bf16.shape[1]
    tn = n // len(epilogues)
    seg = tn // len(epilogues[0])
    n_i, n_j = s // tm, n // tn
    assert b * n_i * n_j >= CAST_SLABS

    def slab(bi, i, j):
        return jnp.minimum((bi * n_i + i) * n_j + j, CAST_SLABS - 1)

    cast_in, cast_out, cast_shape = [], [], []
    for stack, layer in casts:
        _, rows, cols = stack.shape
        cast_in.append(pl.BlockSpec((None, rows // CAST_SLABS, cols),
                                    lambda bi, i, j, layer=layer: (layer, slab(bi, i, j), 0)))
        cast_out.append(pl.BlockSpec((rows // CAST_SLABS, cols),
                                     lambda bi, i, j: (slab(bi, i, j), 0)))
        cast_shape.append(jax.ShapeDtypeStruct((rows, cols), BF16))
    kern = functools.partial(_inproj_kernel, n_casts=len(casts), epilogues=epilogues, seg=seg,
                             normalize=normalize,
                             norm_groups=tuple((r, r + tm // 4) for r in range(0, tm, tm // 4)),
                             row_chunk=BF16_ROWS, out_chunk=256)
    return pl.pallas_call(
        kern,
        grid=(b, s // tm, n // tn),
        in_specs=[
            pl.BlockSpec((None, tm, d), lambda bi, i, j: (bi, i, 0)),
            pl.BlockSpec((1, d), lambda bi, i, j: (0, 0)),
            pl.BlockSpec((None, 1, d), lambda bi, i, j: (bi, 0, 0)),
            pl.BlockSpec((None, 1, d), lambda bi, i, j: (bi, 0, 0)),
            pl.BlockSpec((d, tn), lambda bi, i, j: (0, j)),
            pl.BlockSpec((tm, HEAD_DIM), lambda bi, i, j: (i, 0)),
            pl.BlockSpec((tm, HEAD_DIM), lambda bi, i, j: (i, 0)),
        ] + cast_in,
        out_specs=[pl.BlockSpec((None, tm, tn), lambda bi, i, j: (bi, i, j))] + cast_out,
        out_shape=[jax.ShapeDtypeStruct((b, s, n), BF16)] + cast_shape,
        scratch_shapes=[pltpu.VMEM((tm, d), BF16)] if normalize else [],
        compiler_params=pltpu.CompilerParams(
            dimension_semantics=("arbitrary", "arbitrary", "arbitrary"),
            vmem_limit_bytes=VMEM_LIMIT),
        name="in_projection_ab" if "rope_q" in epilogues[0] else "in_projection_sg",
    )(x, norm_g.reshape(1, d), shift, scale, w_bf16, cos, sin, *[stack for stack, _ in casts])


def _attention_kernel(q_ref, k_ref, v_ref, za_ref, ya_ref,
                      fa, fb, q4, k4, v4, q16, k16, v16, v1, mask_scr):
    s = q_ref.shape[0]
    d4, d16 = DILATIONS[1], DILATIONS[2]
    n4, n16 = s // d4, s // d16
    step = d16 // d4
    chunk = 256

    qi = lax.broadcasted_iota(jnp.int32, (Q_TILE, K_WIN), 0)
    ki = lax.broadcasted_iota(jnp.int32, (Q_TILE, K_WIN), 1)
    for t in range(3):
        mask_scr[t] = jnp.where(jnp.abs(qi + t * RADIUS - ki) <= RADIUS, 0.0, NEG_INF)

    ones = jnp.ones((s, HEAD_DIM), BF16)
    v1[:, HEAD_DIM:2 * HEAD_DIM] = ones
    v4[:, HEAD_DIM:2 * HEAD_DIM] = ones
    v16[:, HEAD_DIM:2 * HEAD_DIM] = ones
    v1[:, 0:HEAD_DIM] = v_ref[...]

    for idx, (src, dst4, dst16) in enumerate(((q_ref, q4, q16), (k_ref, k4, k16),
                                              (v_ref, v4, v16))):
        for c0 in range(0, s, chunk):
            fa[idx, c0:c0 + chunk, :] = src[c0:c0 + chunk, :].astype(F32)
        for r in range(d4):
            for c0 in range(0, n4, chunk):
                part = fa[idx, pl.ds(r + d4 * c0, chunk, stride=d4), :]
                fb[idx, r * n4 + c0:r * n4 + c0 + chunk, :] = part
                dst4[r * n4 + c0:r * n4 + c0 + chunk, 0:HEAD_DIM] = part.astype(BF16)
        for r in range(d4):
            for a in range(step):
                part = fb[idx, pl.ds(r * n4 + a, n16, stride=step), :]
                r16 = r + d4 * a
                dst16[r16 * n16:(r16 + 1) * n16, 0:HEAD_DIM] = part.astype(BF16)

    patterns = ((d16, q16, k16, v16), (d4, q4, k4, v4), (1, q_ref, k_ref, v1))
    states = (None, fa, fb)
    for p, (dil, qs, ks, vs) in enumerate(patterns):
        n = s // dil
        tiles_per_seg = n // Q_TILE

        def one_tile(t, qs=qs, ks=ks, vs=vs, dil=dil, n=n, tiles_per_seg=tiles_per_seg, p=p):
            seg = t // tiles_per_seg
            l0 = (t % tiles_per_seg) * Q_TILE
            kstart = jnp.clip(l0 - RADIUS, 0, n - K_WIN)
            which = (l0 - kstart) // RADIUS
            row0 = pl.multiple_of(seg * n + l0, Q_TILE)
            krow0 = pl.multiple_of(seg * n + kstart, RADIUS)
            rows = pl.ds(row0, Q_TILE)
            qt = qs[rows, :]
            kt = ks[pl.ds(krow0, K_WIN), :]
            vt = vs[pl.ds(krow0, K_WIN), :]
            sc = lax.dot_general(qt, kt, (((1,), (1,)), ((), ())),
                                 preferred_element_type=F32)
            sc = sc + mask_scr[which]
            m = jnp.max(sc, axis=-1, keepdims=True)
            e = jnp.exp2(sc - m).astype(BF16)
            ov = jnp.dot(e, vt, preferred_element_type=F32)
            num = ov[:, 0:HEAD_DIM]
            den = ov[:, HEAD_DIM:2 * HEAD_DIM]
            m = jnp.broadcast_to(m, (Q_TILE, HEAD_DIM))
            if p > 0:
                prev = states[p]
                m_prev = prev[1, rows, :]
                m_all = jnp.maximum(m_prev, m)
                w_prev = jnp.exp2(m_prev - m_all)
                w_cur = jnp.exp2(m - m_all)
                num = w_prev * prev[0, rows, :] + w_cur * num
                den = w_prev * prev[2, rows, :] + w_cur * den
                m = m_all
            if p + 1 < len(patterns):
                nxt, dil_next = states[p + 1], patterns[p + 1][0]
                dst0 = (seg % dil_next) * (s // dil_next) + (dil // dil_next) * l0 + seg // dil_next
                dst = pl.ds(dst0, Q_TILE, stride=dil // dil_next)
                nxt[0, dst, :] = num
                nxt[1, dst, :] = m
                nxt[2, dst, :] = den
            else:
                ya_ref[rows, :] = (num / den * za_ref[rows, :].astype(F32)).astype(BF16)

        def group_body(gi, carry, one_tile=one_tile):
            for u in range(TILE_GROUP):
                one_tile(gi * TILE_GROUP + u)
            return carry

        lax.fori_loop(0, s // (Q_TILE * TILE_GROUP), group_body, 0)


def _attention(proj):
    b, s, _ = proj.shape
    nblk = A_WIDTH // HEAD_DIM

    def col(slot):
        return pl.BlockSpec((None, s, HEAD_DIM), lambda bi, h: (bi, 0, slot * nblk + h))

    return pl.pallas_call(
        _attention_kernel,
        grid=(b, A_HEADS),
        in_specs=[col(0), col(1), col(2), col(3)],
        out_specs=pl.BlockSpec((None, s, HEAD_DIM), lambda bi, h: (bi, 0, h)),
        out_shape=jax.ShapeDtypeStruct((b, s, A_WIDTH), BF16),
        scratch_shapes=[
            pltpu.VMEM((3, s, HEAD_DIM), F32), pltpu.VMEM((3, s, HEAD_DIM), F32),
            pltpu.VMEM((s, HEAD_DIM), BF16), pltpu.VMEM((s, HEAD_DIM), BF16),
            pltpu.VMEM((s, 2 * HEAD_DIM), BF16),
            pltpu.VMEM((s, HEAD_DIM), BF16), pltpu.VMEM((s, HEAD_DIM), BF16),
            pltpu.VMEM((s, 2 * HEAD_DIM), BF16),
            pltpu.VMEM((s, 2 * HEAD_DIM), BF16),
            pltpu.VMEM((3, Q_TILE, K_WIN), F32),
        ],
        compiler_params=pltpu.CompilerParams(
            dimension_semantics=("arbitrary", "arbitrary"),
            vmem_limit_bytes=VMEM_LIMIT),
        name="attention",
    )(proj, proj, proj, proj)


def _ab_out_kernel(ya_ref, ub_ref, gb_ref, gc_ref, zb_ref, ubp_ref, gcp_ref, ubn_ref, gcn_ref,
                   cw_ref, w_ref, x_ref, gate_ref, ng_ref, nshift_ref, nscale_ref,
                   o_ref, *rest, final, row_groups):
    h_ref, pbuf, yb_scr = ((None,) + rest) if final else rest
    i = pl.program_id(1)
    tm = x_ref.shape[0]
    halo = ubp_ref.shape[0]
    chunk = 32

    prev = (gcp_ref[...].astype(F32) * ubp_ref[...].astype(F32))[halo - 1:halo, :]
    nxt = (gcn_ref[...].astype(F32) * ubn_ref[...].astype(F32))[0:1, :]
    pbuf[7:8, :] = jnp.where(i > 0, prev, 0.0)
    pbuf[8 + tm:9 + tm, :] = jnp.where(i < pl.num_programs(1) - 1, nxt, 0.0)
    for c0 in range(0, tm, chunk):
        rows = slice(c0, c0 + chunk)
        pbuf[8 + c0:8 + c0 + chunk, :] = gc_ref[rows, :].astype(F32) * ub_ref[rows, :].astype(F32)
    w0 = cw_ref[0:1, :]
    w1 = cw_ref[1:2, :]
    w2 = cw_ref[2:3, :]

    for d0, d1 in row_groups:
        for c0 in range(d0, d1, chunk):
            rows = slice(c0, c0 + chunk)
            conv = (w0 * pbuf[7 + c0:7 + c0 + chunk, :] + w1 * pbuf[8 + c0:8 + c0 + chunk, :]
                    + w2 * pbuf[9 + c0:9 + c0 + chunk, :])
            gate = gb_ref[rows, :].astype(F32) * zb_ref[rows, :].astype(F32)
            yb_scr[rows, :] = (gate * conv).astype(BF16)
        rows = slice(d0, d1)
        out = jnp.dot(ya_ref[rows, :], w_ref[0:A_WIDTH, :], preferred_element_type=F32)
        out = out + jnp.dot(yb_scr[rows, :], w_ref[A_WIDTH:A_WIDTH + B_WIDTH, :],
                            preferred_element_type=F32)
        xn = x_ref[rows, :] + gate_ref[...] * out
        _residual_tail(xn, rows, ng_ref, nshift_ref, nscale_ref, o_ref, h_ref, final)


def _residual_out(b, s, d, tm, final):
    spec = pl.BlockSpec((None, tm, d), lambda bi, i: (bi, i, 0))
    if final:
        return spec, jax.ShapeDtypeStruct((b, s, d), F32)
    return [spec, spec], [jax.ShapeDtypeStruct((b, s, d), F32),
                          jax.ShapeDtypeStruct((b, s, d), BF16)]


def _ab_out_projection(ya, proj, conv_w, w_bf16, x, gate, next_g, next_shift, next_scale, *,
                       final, tm=512, halo=BF16_ROWS):
    b, s, d = x.shape
    out_specs, out_shape = _residual_out(b, s, d, tm, final)
    row_vec = pl.BlockSpec((None, 1, d), lambda bi, i: (bi, 0, 0))
    per_tile = tm // halo
    n_halo = s // halo

    def slot(k):
        return pl.BlockSpec((None, tm, B_WIDTH), lambda bi, i: (bi, i, k))

    def before(k):
        return pl.BlockSpec((None, halo, B_WIDTH),
                            lambda bi, i: (bi, jnp.maximum(i * per_tile - 1, 0), k))

    def after(k):
        return pl.BlockSpec((None, halo, B_WIDTH),
                            lambda bi, i: (bi, jnp.minimum((i + 1) * per_tile, n_halo - 1), k))

    return pl.pallas_call(
        functools.partial(_ab_out_kernel, final=final, row_groups=_row_groups(tm)),
        grid=(b, s // tm),
        in_specs=[
            pl.BlockSpec((None, tm, A_WIDTH), lambda bi, i: (bi, i, 0)),
            slot(4), slot(5), slot(6), slot(7),
            before(4), before(6), after(4), after(6),
            pl.BlockSpec((B_CONV, B_WIDTH), lambda bi, i: (0, 0)),
            pl.BlockSpec((A_WIDTH + B_WIDTH, d), lambda bi, i: (0, 0),
                         pipeline_mode=pl.Buffered(1)),
            pl.BlockSpec((None, tm, d), lambda bi, i: (bi, i, 0)),
            row_vec,
            pl.BlockSpec((1, d), lambda bi, i: (0, 0)),
            row_vec,
            row_vec,
        ],
        out_specs=out_specs,
        out_shape=out_shape,
        scratch_shapes=[pltpu.VMEM((tm + 16, B_WIDTH), F32), pltpu.VMEM((tm, B_WIDTH), BF16)],
        compiler_params=pltpu.CompilerParams(
            dimension_semantics=("arbitrary", "arbitrary"),
            vmem_limit_bytes=VMEM_LIMIT),
        name="ab_out_projection",
    )(ya, proj, proj, proj, proj, proj, proj, proj, proj, conv_w, w_bf16, x, gate,
      next_g.reshape(1, d), next_shift, next_scale)


def _sgu_kernel(u_ref, v_ref, z_ref, lng_ref, lnb_ref, ws_ref, bs_ref, w_ref, x_ref, gate_ref,
                ng_ref, nshift_ref, nscale_ref, o_ref, *rest, final, row_groups):
    h_ref, y_scr = (None, rest[0]) if final else rest
    tm = u_ref.shape[0]
    gw = C_WIDTH // C_GROUPS
    lng = lng_ref[...]
    lnb = lnb_ref[...]
    for d0, d1 in row_groups:
        for c0 in range(d0, d1, C_CHUNK):
            rows = slice(c0, c0 + C_CHUNK)
            v = v_ref[rows, :].astype(F32)
            mu = jnp.mean(v, axis=-1, keepdims=True)
            vc = v - mu
            var = jnp.mean(vc * vc, axis=-1, keepdims=True)
            vn = (vc * lax.rsqrt(var + EPS) * lng + lnb).astype(BF16)
            for g in range(C_GROUPS):
                cols = slice(g * gw, (g + 1) * gw)
                mixed = jnp.dot(ws_ref[g], vn[:, cols], preferred_element_type=F32)
                mixed = mixed + bs_ref[:, g:g + 1]
                gated = u_ref[rows, cols].astype(F32) * z_ref[rows, cols].astype(F32)
                y_scr[rows, cols] = (gated * mixed).astype(BF16)
        rows = slice(d0, d1)
        out = jnp.dot(y_scr[rows, :], w_ref[...], preferred_element_type=F32)
        xn = x_ref[rows, :] + gate_ref[...] * out
        _residual_tail(xn, rows, ng_ref, nshift_ref, nscale_ref, o_ref, h_ref, final)


def _sgu(proj, ln_g, ln_b, ws_bf16, bs_t, w_bf16, x, gate, next_g, next_shift, next_scale, *,
         final, tm=512):
    b, s, d = x.shape
    cw = C_WIDTH
    kern = functools.partial(_sgu_kernel, final=final, row_groups=_row_groups(tm))
    out_specs, out_shape = _residual_out(b, s, d, tm, final)
    row_vec = pl.BlockSpec((None, 1, d), lambda bi, i: (bi, 0, 0))
    return pl.pallas_call(
        kern,
        grid=(b, s // tm),
        in_specs=[
            pl.BlockSpec((None, tm, cw), lambda bi, i: (bi, i, 0)),
            pl.BlockSpec((None, tm, cw), lambda bi, i: (bi, i, 1)),
            pl.BlockSpec((None, tm, cw), lambda bi, i: (bi, i, 2)),
            pl.BlockSpec((1, cw), lambda bi, i: (0, 0)),
            pl.BlockSpec((1, cw), lambda bi, i: (0, 0)),
            pl.BlockSpec((C_GROUPS, C_CHUNK, C_CHUNK), lambda bi, i: (0, 0, 0)),
            pl.BlockSpec((C_CHUNK, C_GROUPS), lambda bi, i: (0, 0)),
            pl.BlockSpec((cw, d), lambda bi, i: (0, 0), pipeline_mode=pl.Buffered(1)),
            pl.BlockSpec((None, tm, d), lambda bi, i: (bi, i, 0)),
            row_vec,
            pl.BlockSpec((1, d), lambda bi, i: (0, 0)),
            row_vec,
            row_vec,
        ],
        out_specs=out_specs,
        out_shape=out_shape,
        scratch_shapes=[pltpu.VMEM((tm, cw), BF16)],
        compiler_params=pltpu.CompilerParams(
            dimension_semantics=("arbitrary", "arbitrary"),
            vmem_limit_bytes=VMEM_LIMIT),
        name="sgu_final" if final else "sgu",
    )(proj, proj, proj, ln_g.reshape(1, cw), ln_b.reshape(1, cw), ws_bf16, bs_t, w_bf16, x,
      gate, next_g.reshape(1, d), next_shift, next_scale)


def _rope_tables(s):
    half = HEAD_DIM // 2
    inv = ROPE_THETA ** (-jnp.arange(half, dtype=F32) / half)
    ang = jnp.arange(s, dtype=F32)[:, None] * inv[None, :]
    cos = jnp.cos(ang)
    sin = jnp.sin(ang)
    return (jnp.concatenate([cos, cos], axis=-1), jnp.concatenate([-sin, sin], axis=-1))


def _split_mod(mod, batch):
    m = mod[:batch].reshape(batch, 1, 3, D_MODEL)
    return m[:, :, 0, :], m[:, :, 1, :], m[:, :, 2, :]


def kernel(x, c, ab_norm_g, ab_w_mod, ab_b_mod, ab_w_in, ab_conv_w, ab_w_out, sg_norm_g, sg_w_mod, sg_b_mod, sg_w_in, sg_ln_g, sg_ln_b, sg_w_s, sg_b_s, sg_w_out, final_norm_g):
    batch, s, _ = x.shape
    depth = ab_w_in.shape[0] + sg_w_in.shape[0]
    c_pad = jnp.pad(c, ((0, BF16_ROWS - batch), (0, 0)))
    mod_ab = _modulation(c_pad, ab_w_mod, ab_b_mod)
    mod_sg = _modulation(c_pad, sg_w_mod, sg_b_mod)
    cos, sin = _rope_tables(s)

    w_in_b = ab_w_in[0].astype(BF16)
    sg_w_s_b = sg_w_s.astype(BF16)

    params = []
    for layer in range(depth):
        i = layer // 2
        norm_g, mod = (ab_norm_g, mod_ab) if layer % 2 == 0 else (sg_norm_g, mod_sg)
        params.append((norm_g[i],) + _split_mod(mod[i], batch))

    h = None
    for layer in range(depth):
        i = layer // 2
        norm_g, shift, scale, gate = params[layer]
        final = layer == depth - 1
        nxt = (final_norm_g, gate, gate) if final else params[layer + 1][:3]
        src, normalize = (x, True) if h is None else (h, False)
        if layer % 2 == 0:
            casts = ((ab_w_out, i),) + (() if final else ((sg_w_in, i),))
            proj, w_out_b, *w_next = _in_projection(src, norm_g, shift, scale, w_in_b, cos, sin,
                                                    epilogues=AB_EPILOGUES, normalize=normalize,
                                                    casts=casts)
            ya = _attention(proj)
            res = _ab_out_projection(ya, proj, ab_conv_w[i], w_out_b, x, gate, *nxt, final=final)
        else:
            casts = ((sg_w_out, i),) + (() if final else ((ab_w_in, i + 1),))
            proj, w_out_b, *w_next = _in_projection(src, norm_g, shift, scale, w_in_b, cos, sin,
                                                    epilogues=SG_EPILOGUES, normalize=normalize,
                                                    casts=casts)
            res = _sgu(proj, sg_ln_g[i], sg_ln_b[i], sg_w_s_b[i], sg_b_s[i].T, w_out_b, x, gate,
                       *nxt, final=final)
        if final:
            return res
        x, h = res
        w_in_b, = w_next
```

```python
import functools

import jax
import jax.numpy as jnp
import numpy as np
from jax import lax
from jax.experimental import pallas as pl
from jax.experimental.pallas import tpu as pltpu

F32 = jnp.float32
BF16 = jnp.bfloat16

D_MODEL = 2048
HEAD_DIM = 128
A_WIDTH = 1024
A_HEADS = 8
B_WIDTH = 1024
B_CONV = 3
DILATIONS = (1, 4, 16)
RADIUS = 64
ROPE_THETA = 10000.0
NEG_INF = -1e30
C_WIDTH = 2048
C_GROUPS = 8
C_CHUNK = 128
EPS = 1e-6

LOG2_E = 1.4426950408889634
Q_SCALE = HEAD_DIM ** -0.5 * LOG2_E
Q_TILE = 128
K_WIN = Q_TILE + 2 * RADIUS
TILE_GROUP = 32
CAST_SLABS = 32
MOD_COLS = 128
BF16_ROWS = 16
VMEM_LIMIT = 60 * 1024 * 1024

AB_EPILOGUES = (("rope_q", "rope_k"), ("none", "silu"), ("none", "none"), ("none", "silu"))
SG_EPILOGUES = (("gelu", "gelu"), ("gelu", "gelu"), ("silu", "silu"))


def _silu(z):
    hz = 0.5 * z
    return hz + hz * jnp.tanh(hz)


def _gelu_tanh(x):
    c = 0.7978845608028654
    hx = 0.5 * x
    return hx + hx * jnp.tanh(x * (c + (c * 0.044715) * (x * x)))


def _rms_norm(x, gain):
    ms = jnp.mean(x * x, axis=-1, keepdims=True)
    return x * lax.rsqrt(ms + EPS) * gain


def _modulated_rms_norm(x, gm, shift):
    return _rms_norm(x, gm) + shift


def _row_groups(tm):
    edges = (0, tm // 2, tm)
    return tuple(zip(edges[:-1], edges[1:]))


def _residual_tail(xn, rows, ng_ref, nshift_ref, nscale_ref, o_ref, h_ref, final):
    if final:
        o_ref[rows, :] = _rms_norm(xn, ng_ref[...])
    else:
        o_ref[rows, :] = xn
        gm = ng_ref[...] * (1.0 + nscale_ref[...])
        h_ref[rows, :] = _modulated_rms_norm(xn, gm, nshift_ref[...]).astype(BF16)


def _mod_kernel(c_ref, w_ref, b_ref, o_ref):
    a = _silu(c_ref[...]).astype(BF16)
    w = w_ref[...].astype(BF16)
    o_ref[...] = jnp.dot(a, w, preferred_element_type=F32) + b_ref[...]


def _modulation(c_pad, w_mod, b_mod, layer, tn=1024):
    _, d, n = w_mod.shape
    rows = c_pad.shape[0]
    return pl.pallas_call(
        _mod_kernel,
        grid=(n // tn,),
        in_specs=[
            pl.BlockSpec((rows, d), lambda j: (0, 0)),
            pl.BlockSpec((None, d, tn), lambda j: (layer, 0, j)),
            pl.BlockSpec((None, 1, tn), lambda j: (layer, 0, j)),
        ],
        out_specs=pl.BlockSpec((rows, tn), lambda j: (0, j)),
        out_shape=jax.ShapeDtypeStruct((rows, n), F32),
        compiler_params=pltpu.CompilerParams(
            dimension_semantics=("arbitrary",),
            vmem_limit_bytes=VMEM_LIMIT),
        name="modulation",
    )(c_pad, w_mod, b_mod.reshape(b_mod.shape[0], 1, n))


def _inproj_kernel(x_ref, g_ref, shift_ref, scale_ref, w_ref, cos_ref, sin_ref, *rest,
                   n_casts, has_mod, epilogues, seg, normalize, norm_groups, row_chunk, out_chunk):
    j = pl.program_id(2)
    tm = x_ref.shape[0]
    n_mod = 3 if has_mod else 0
    cast_src, mod_src = rest[:n_casts], rest[n_casts:n_casts + n_mod]
    o_ref, rest = rest[n_casts + n_mod], rest[n_casts + n_mod + 1:]
    n_dst = n_casts + n_mod // 3
    cast_dst, mod_dst, scratch = rest[:n_casts], rest[n_casts:n_dst], rest[n_dst:]
    h_ref = scratch[0] if normalize else x_ref

    def normalize_rows(r0, r1):
        gm = g_ref[...] * (1.0 + scale_ref[...])
        shift = shift_ref[...]
        for c0 in range(r0, r1, row_chunk):
            rows = slice(c0, c0 + row_chunk)
            h_ref[rows, :] = _modulated_rms_norm(x_ref[rows, :], gm, shift).astype(BF16)

    def rope(acc, r0, c0, f):
        rows = slice(r0, r0 + acc.shape[0])
        cos = cos_ref[rows, :] * f
        sin = sin_ref[rows, :] * f
        for hh in range(seg // HEAD_DIM):
            cols = slice(c0 + hh * HEAD_DIM, c0 + (hh + 1) * HEAD_DIM)
            blk = acc[:, cols]
            o_ref[rows, cols] = (blk * cos + pltpu.roll(blk, HEAD_DIM // 2, 1) * sin).astype(BF16)

    def pointwise(acc, r0, c0, fn):
        for q0 in range(0, acc.shape[0], out_chunk):
            o_ref[r0 + q0:r0 + q0 + out_chunk, c0:c0 + seg] = fn(
                acc[q0:q0 + out_chunk, c0:c0 + seg]).astype(BF16)

    finish = {
        "rope_q": lambda acc, r0, c0: rope(acc, r0, c0, Q_SCALE),
        "rope_k": lambda acc, r0, c0: rope(acc, r0, c0, 1.0),
        "none": lambda acc, r0, c0: pointwise(acc, r0, c0, lambda t: t),
        "silu": lambda acc, r0, c0: pointwise(acc, r0, c0, _silu),
        "gelu": lambda acc, r0, c0: pointwise(acc, r0, c0, _gelu_tanh),
    }

    def project(kinds, groups, with_norm):
        for src, dst in zip(cast_src, cast_dst):
            dst[...] = src[...].astype(BF16)
        if has_mod:
            _mod_kernel(*mod_src, *mod_dst)
        for r0, r1 in groups:
            if with_norm:
                normalize_rows(r0, r1)
            acc = jnp.dot(h_ref[r0:r1, :], w_ref[...], preferred_element_type=F32)
            for si, kind in enumerate(kinds):
                finish[kind](acc, r0, si * seg)

    plain_tiles = list(enumerate(epilogues))
    if normalize:
        plain_tiles = plain_tiles[1:]

        @pl.when(j == 0)
        def _():
            project(epilogues[0], norm_groups, True)

    for kinds in sorted(set(e for _, e in plain_tiles)):
        tiles = [t for t, e in plain_tiles if e == kinds]
        cond = functools.reduce(jnp.logical_or, [j == t for t in tiles])

        @pl.when(cond)
        def _(kinds=kinds):
            project(kinds, ((0, tm),), False)


def _in_projection(x, norm_g, shift, scale, w_bf16, cos, sin, *, epilogues, normalize, casts=(),
                   mod=None, tm=1024):
    b, s, d = x.shape
    n = w_bf16.shape[1]
    tn = n // len(epilogues)
    seg = tn // len(epilogues[0])
    n_i, n_j = s // tm, n // tn
    n_steps = b * n_i * n_j
    assert n_steps >= CAST_SLABS

    def step(bi, i, j):
        return (bi * n_i + i) * n_j + j

    def slab(bi, i, j):
        return jnp.minimum(step(bi, i, j), CAST_SLABS - 1)

    cast_in, cast_out, cast_shape = [], [], []
    for stack, layer in casts:
        _, rows, cols = stack.shape
        cast_in.append(pl.BlockSpec((None, rows // CAST_SLABS, cols),
                                    lambda bi, i, j, layer=layer: (layer, slab(bi, i, j), 0)))
        cast_out.append(pl.BlockSpec((rows // CAST_SLABS, cols),
                                     lambda bi, i, j: (slab(bi, i, j), 0)))
        cast_shape.append(jax.ShapeDtypeStruct((rows, cols), BF16))
    mod_in, mod_out, mod_shape, mod_args = [], [], [], []
    if mod is not None:
        c_pad, w_mod, b_mod, mod_layer = mod
        n_mod = w_mod.shape[2]
        mod_slabs = n_mod // MOD_COLS
        assert n_steps >= mod_slabs

        def mod_slab(bi, i, j):
            return jnp.minimum(step(bi, i, j), mod_slabs - 1)

        mod_in = [pl.BlockSpec(c_pad.shape, lambda bi, i, j: (0, 0)),
                  pl.BlockSpec((None, d, MOD_COLS),
                               lambda bi, i, j: (mod_layer, 0, mod_slab(bi, i, j))),
                  pl.BlockSpec((None, 1, MOD_COLS),
                               lambda bi, i, j: (mod_layer, 0, mod_slab(bi, i, j)))]
        mod_out = [pl.BlockSpec((c_pad.shape[0], MOD_COLS),
                                lambda bi, i, j: (0, mod_slab(bi, i, j)))]
        mod_shape = [jax.ShapeDtypeStruct((c_pad.shape[0], n_mod), F32)]
        mod_args = [c_pad, w_mod, b_mod.reshape(b_mod.shape[0], 1, n_mod)]
    kern = functools.partial(_inproj_kernel, n_casts=len(casts), has_mod=mod is not None,
                             epilogues=epilogues, seg=seg,
                             normalize=normalize,
                             norm_groups=tuple((r, r + tm // 4) for r in range(0, tm, tm // 4)),
                             row_chunk=BF16_ROWS, out_chunk=256)
    return pl.pallas_call(
        kern,
        grid=(b, s // tm, n // tn),
        in_specs=[
            pl.BlockSpec((None, tm, d), lambda bi, i, j: (bi, i, 0)),
            pl.BlockSpec((1, d), lambda bi, i, j: (0, 0)),
            pl.BlockSpec((None, 1, d), lambda bi, i, j: (bi, 0, 0)),
            pl.BlockSpec((None, 1, d), lambda bi, i, j: (bi, 0, 0)),
            pl.BlockSpec((d, tn), lambda bi, i, j: (0, j)),
            pl.BlockSpec((tm, HEAD_DIM), lambda bi, i, j: (i, 0)),
            pl.BlockSpec((tm, HEAD_DIM), lambda bi, i, j: (i, 0)),
        ] + cast_in + mod_in,
        out_specs=[pl.BlockSpec((None, tm, tn), lambda bi, i, j: (bi, i, j))] + cast_out + mod_out,
        out_shape=[jax.ShapeDtypeStruct((b, s, n), BF16)] + cast_shape + mod_shape,
        scratch_shapes=[pltpu.VMEM((tm, d), BF16)] if normalize else [],
        compiler_params=pltpu.CompilerParams(
            dimension_semantics=("arbitrary", "arbitrary", "arbitrary"),
            vmem_limit_bytes=VMEM_LIMIT),
        name="in_projection_ab" if "rope_q" in epilogues[0] else "in_projection_sg",
    )(x, norm_g.reshape(1, d), shift, scale, w_bf16, cos, sin, *[stack for stack, _ in casts],
      *mod_args)


def _attention_kernel(q_ref, k_ref, v_ref, za_ref, ya_ref,
                      fa, fb, q4, k4, v4, q16, k16, v16, v1, mask_scr):
    s = q_ref.shape[0]
    d4, d16 = DILATIONS[1], DILATIONS[2]
    n4, n16 = s // d4, s // d16
    step = d16 // d4
    chunk = 256

    qi = lax.broadcasted_iota(jnp.int32, (Q_TILE, K_WIN), 0)
    ki = lax.broadcasted_iota(jnp.int32, (Q_TILE, K_WIN), 1)
    for t in range(3):
        mask_scr[t] = jnp.where(jnp.abs(qi + t * RADIUS - ki) <= RADIUS, 0.0, NEG_INF)

    ones = jnp.ones((s, HEAD_DIM), BF16)
    v1[:, HEAD_DIM:2 * HEAD_DIM] = ones
    v4[:, HEAD_DIM:2 * HEAD_DIM] = ones
    v16[:, HEAD_DIM:2 * HEAD_DIM] = ones
    v1[:, 0:HEAD_DIM] = v_ref[...]

    for idx, (src, dst4, dst16) in enumerate(((q_ref, q4, q16), (k_ref, k4, k16),
                                              (v_ref, v4, v16))):
        for c0 in range(0, s, chunk):
            fa[idx, c0:c0 + chunk, :] = src[c0:c0 + chunk, :].astype(F32)
        for r in range(d4):
            for c0 in range(0, n4, chunk):
                part = fa[idx, pl.ds(r + d4 * c0, chunk, stride=d4), :]
                fb[idx, r * n4 + c0:r * n4 + c0 + chunk, :] = part
                dst4[r * n4 + c0:r * n4 + c0 + chunk, 0:HEAD_DIM] = part.astype(BF16)
        for r in range(d4):
            for a in range(step):
                part = fb[idx, pl.ds(r * n4 + a, n16, stride=step), :]
                r16 = r + d4 * a
                dst16[r16 * n16:(r16 + 1) * n16, 0:HEAD_DIM] = part.astype(BF16)

    patterns = ((d16, q16, k16, v16), (d4, q4, k4, v4), (1, q_ref, k_ref, v1))
    states = (None, fa, fb)
    for p, (dil, qs, ks, vs) in enumerate(patterns):
        n = s // dil
        tiles_per_seg = n // Q_TILE

        def one_tile(t, qs=qs, ks=ks, vs=vs, dil=dil, n=n, tiles_per_seg=tiles_per_seg, p=p):
            seg = t // tiles_per_seg
            l0 = (t % tiles_per_seg) * Q_TILE
            kstart = jnp.clip(l0 - RADIUS, 0, n - K_WIN)
            which = (l0 - kstart) // RADIUS
            row0 = pl.multiple_of(seg * n + l0, Q_TILE)
            krow0 = pl.multiple_of(seg * n + kstart, RADIUS)
            rows = pl.ds(row0, Q_TILE)
            qt = qs[rows, :]
            kt = ks[pl.ds(krow0, K_WIN), :]
            vt = vs[pl.ds(krow0, K_WIN), :]
            sc = lax.dot_general(qt, kt, (((1,), (1,)), ((), ())),
                                 preferred_element_type=F32)
            sc = sc + mask_scr[which]
            m = jnp.max(sc, axis=-1, keepdims=True)
            e = jnp.exp2(sc - m).astype(BF16)
            ov = jnp.dot(e, vt, preferred_element_type=F32)
            num = ov[:, 0:HEAD_DIM]
            den = ov[:, HEAD_DIM:2 * HEAD_DIM]
            m = jnp.broadcast_to(m, (Q_TILE, HEAD_DIM))
            if p > 0:
                prev = states[p]
                m_prev = prev[1, rows, :]
                m_all = jnp.maximum(m_prev, m)
                w_prev = jnp.exp2(m_prev - m_all)
                w_cur = jnp.exp2(m - m_all)
                num = w_prev * prev[0, rows, :] + w_cur * num
                den = w_prev * prev[2, rows, :] + w_cur * den
                m = m_all
            if p + 1 < len(patterns):
                nxt, dil_next = states[p + 1], patterns[p + 1][0]
                dst0 = (seg % dil_next) * (s // dil_next) + (dil // dil_next) * l0 + seg // dil_next
                dst = pl.ds(dst0, Q_TILE, stride=dil // dil_next)
                nxt[0, dst, :] = num
                nxt[1, dst, :] = m
                nxt[2, dst, :] = den
            else:
                ya_ref[rows, :] = (num / den * za_ref[rows, :].astype(F32)).astype(BF16)

        def group_body(gi, carry, one_tile=one_tile):
            for u in range(TILE_GROUP):
                one_tile(gi * TILE_GROUP + u)
            return carry

        lax.fori_loop(0, s // (Q_TILE * TILE_GROUP), group_body, 0)


def _attention(proj):
    b, s, _ = proj.shape
    nblk = A_WIDTH // HEAD_DIM

    def col(slot):
        return pl.BlockSpec((None, s, HEAD_DIM), lambda bi, h: (bi, 0, slot * nblk + h))

    return pl.pallas_call(
        _attention_kernel,
        grid=(b, A_HEADS),
        in_specs=[col(0), col(1), col(2), col(3)],
        out_specs=pl.BlockSpec((None, s, HEAD_DIM), lambda bi, h: (bi, 0, h)),
        out_shape=jax.ShapeDtypeStruct((b, s, A_WIDTH), BF16),
        scratch_shapes=[
            pltpu.VMEM((3, s, HEAD_DIM), F32), pltpu.VMEM((3, s, HEAD_DIM), F32),
            pltpu.VMEM((s, HEAD_DIM), BF16), pltpu.VMEM((s, HEAD_DIM), BF16),
            pltpu.VMEM((s, 2 * HEAD_DIM), BF16),
            pltpu.VMEM((s, HEAD_DIM), BF16), pltpu.VMEM((s, HEAD_DIM), BF16),
            pltpu.VMEM((s, 2 * HEAD_DIM), BF16),
            pltpu.VMEM((s, 2 * HEAD_DIM), BF16),
            pltpu.VMEM((3, Q_TILE, K_WIN), F32),
        ],
        compiler_params=pltpu.CompilerParams(
            dimension_semantics=("arbitrary", "arbitrary"),
            vmem_limit_bytes=VMEM_LIMIT),
        name="attention",
    )(proj, proj, proj, proj)


def _ab_out_kernel(ya_ref, ub_ref, gb_ref, gc_ref, zb_ref, ubp_ref, gcp_ref, ubn_ref, gcn_ref,
                   cw_ref, w_ref, x_ref, gate_ref, ng_ref, nshift_ref, nscale_ref,
                   o_ref, *rest, final, row_groups):
    h_ref, pbuf, yb_scr = ((None,) + rest) if final else rest
    i = pl.program_id(1)
    tm = x_ref.shape[0]
    halo = ubp_ref.shape[0]
    chunk = 32

    prev = (gcp_ref[...].astype(F32) * ubp_ref[...].astype(F32))[halo - 1:halo, :]
    nxt = (gcn_ref[...].astype(F32) * ubn_ref[...].astype(F32))[0:1, :]
    pbuf[7:8, :] = jnp.where(i > 0, prev, 0.0)
    pbuf[8 + tm:9 + tm, :] = jnp.where(i < pl.num_programs(1) - 1, nxt, 0.0)
    for c0 in range(0, tm, chunk):
        rows = slice(c0, c0 + chunk)
        pbuf[8 + c0:8 + c0 + chunk, :] = gc_ref[rows, :].astype(F32) * ub_ref[rows, :].astype(F32)
    w0 = cw_ref[0:1, :]
    w1 = cw_ref[1:2, :]
    w2 = cw_ref[2:3, :]

    for d0, d1 in row_groups:
        for c0 in range(d0, d1, chunk):
            rows = slice(c0, c0 + chunk)
            conv = (w0 * pbuf[7 + c0:7 + c0 + chunk, :] + w1 * pbuf[8 + c0:8 + c0 + chunk, :]
                    + w2 * pbuf[9 + c0:9 + c0 + chunk, :])
            gate = gb_ref[rows, :].astype(F32) * zb_ref[rows, :].astype(F32)
            yb_scr[rows, :] = (gate * conv).astype(BF16)
        rows = slice(d0, d1)
        out = jnp.dot(ya_ref[rows, :], w_ref[0:A_WIDTH, :], preferred_element_type=F32)
        out = out + jnp.dot(yb_scr[rows, :], w_ref[A_WIDTH:A_WIDTH + B_WIDTH, :],
                            preferred_element_type=F32)
        xn = x_ref[rows, :] + gate_ref[...] * out
        _residual_tail(xn, rows, ng_ref, nshift_ref, nscale_ref, o_ref, h_ref, final)


def _residual_out(b, s, d, tm, final):
    spec = pl.BlockSpec((None, tm, d), lambda bi, i: (bi, i, 0))
    if final:
        return spec, jax.ShapeDtypeStruct((b, s, d), F32)
    return [spec, spec], [jax.ShapeDtypeStruct((b, s, d), F32),
                          jax.ShapeDtypeStruct((b, s, d), BF16)]


def _ab_out_projection(ya, proj, conv_w, w_bf16, x, gate, next_g, next_shift, next_scale, *,
                       final, tm=512, halo=BF16_ROWS):
    b, s, d = x.shape
    out_specs, out_shape = _residual_out(b, s, d, tm, final)
    row_vec = pl.BlockSpec((None, 1, d), lambda bi, i: (bi, 0, 0))
    per_tile = tm // halo
    n_halo = s // halo

    def slot(k):
        return pl.BlockSpec((None, tm, B_WIDTH), lambda bi, i: (bi, i, k))

    def before(k):
        return pl.BlockSpec((None, halo, B_WIDTH),
                            lambda bi, i: (bi, jnp.maximum(i * per_tile - 1, 0), k))

    def after(k):
        return pl.BlockSpec((None, halo, B_WIDTH),
                            lambda bi, i: (bi, jnp.minimum((i + 1) * per_tile, n_halo - 1), k))

    return pl.pallas_call(
        functools.partial(_ab_out_kernel, final=final, row_groups=_row_groups(tm)),
        grid=(b, s // tm),
        in_specs=[
            pl.BlockSpec((None, tm, A_WIDTH), lambda bi, i: (bi, i, 0)),
            slot(4), slot(5), slot(6), slot(7),
            before(4), before(6), after(4), after(6),
            pl.BlockSpec((B_CONV, B_WIDTH), lambda bi, i: (0, 0)),
            pl.BlockSpec((A_WIDTH + B_WIDTH, d), lambda bi, i: (0, 0),
                         pipeline_mode=pl.Buffered(1)),
            pl.BlockSpec((None, tm, d), lambda bi, i: (bi, i, 0)),
            row_vec,
            pl.BlockSpec((1, d), lambda bi, i: (0, 0)),
            row_vec,
            row_vec,
        ],
        out_specs=out_specs,
        out_shape=out_shape,
        scratch_shapes=[pltpu.VMEM((tm + 16, B_WIDTH), F32), pltpu.VMEM((tm, B_WIDTH), BF16)],
        compiler_params=pltpu.CompilerParams(
            dimension_semantics=("arbitrary", "arbitrary"),
            vmem_limit_bytes=VMEM_LIMIT),
        name="ab_out_projection",
    )(ya, proj, proj, proj, proj, proj, proj, proj, proj, conv_w, w_bf16, x, gate,
      next_g.reshape(1, d), next_shift, next_scale)


def _sgu_kernel(u_ref, v_ref, z_ref, lng_ref, lnb_ref, ws_ref, bs_ref, w_ref, x_ref, gate_ref,
                ng_ref, nshift_ref, nscale_ref, o_ref, *rest, final, row_groups):
    h_ref, y_scr = (None, rest[0]) if final else rest
    tm = u_ref.shape[0]
    gw = C_WIDTH // C_GROUPS
    lng = lng_ref[...]
    lnb = lnb_ref[...]
    for d0, d1 in row_groups:
        for c0 in range(d0, d1, C_CHUNK):
            rows = slice(c0, c0 + C_CHUNK)
            v = v_ref[rows, :].astype(F32)
            mu = jnp.mean(v, axis=-1, keepdims=True)
            vc = v - mu
            var = jnp.mean(vc * vc, axis=-1, keepdims=True)
            vn = (vc * lax.rsqrt(var + EPS) * lng + lnb).astype(BF16)
            for g in range(C_GROUPS):
                cols = slice(g * gw, (g + 1) * gw)
                mixed = jnp.dot(ws_ref[g], vn[:, cols], preferred_element_type=F32)
                mixed = mixed + bs_ref[:, g:g + 1]
                gated = u_ref[rows, cols].astype(F32) * z_ref[rows, cols].astype(F32)
                y_scr[rows, cols] = (gated * mixed).astype(BF16)
        rows = slice(d0, d1)
        out = jnp.dot(y_scr[rows, :], w_ref[...], preferred_element_type=F32)
        xn = x_ref[rows, :] + gate_ref[...] * out
        _residual_tail(xn, rows, ng_ref, nshift_ref, nscale_ref, o_ref, h_ref, final)


def _sgu(proj, ln_g, ln_b, ws_bf16, bs_t, w_bf16, x, gate, next_g, next_shift, next_scale, *,
         final, tm=512):
    b, s, d = x.shape
    cw = C_WIDTH
    kern = functools.partial(_sgu_kernel, final=final, row_groups=_row_groups(tm))
    out_specs, out_shape = _residual_out(b, s, d, tm, final)
    row_vec = pl.BlockSpec((None, 1, d), lambda bi, i: (bi, 0, 0))
    return pl.pallas_call(
        kern,
        grid=(b, s // tm),
        in_specs=[
            pl.BlockSpec((None, tm, cw), lambda bi, i: (bi, i, 0)),
            pl.BlockSpec((None, tm, cw), lambda bi, i: (bi, i, 1)),
            pl.BlockSpec((None, tm, cw), lambda bi, i: (bi, i, 2)),
            pl.BlockSpec((1, cw), lambda bi, i: (0, 0)),
            pl.BlockSpec((1, cw), lambda bi, i: (0, 0)),
            pl.BlockSpec((C_GROUPS, C_CHUNK, C_CHUNK), lambda bi, i: (0, 0, 0)),
            pl.BlockSpec((C_CHUNK, C_GROUPS), lambda bi, i: (0, 0)),
            pl.BlockSpec((cw, d), lambda bi, i: (0, 0), pipeline_mode=pl.Buffered(1)),
            pl.BlockSpec((None, tm, d), lambda bi, i: (bi, i, 0)),
            row_vec,
            pl.BlockSpec((1, d), lambda bi, i: (0, 0)),
            row_vec,
            row_vec,
        ],
        out_specs=out_specs,
        out_shape=out_shape,
        scratch_shapes=[pltpu.VMEM((tm, cw), BF16)],
        compiler_params=pltpu.CompilerParams(
            dimension_semantics=("arbitrary", "arbitrary"),
            vmem_limit_bytes=VMEM_LIMIT),
        name="sgu_final" if final else "sgu",
    )(proj, proj, proj, ln_g.reshape(1, cw), ln_b.reshape(1, cw), ws_bf16, bs_t, w_bf16, x,
      gate, next_g.reshape(1, d), next_shift, next_scale)


def _rope_tables(s):
    half = HEAD_DIM // 2
    inv = np.float32(ROPE_THETA) ** (-np.arange(half, dtype=np.float32) / np.float32(half))
    ang = np.arange(s, dtype=np.float32)[:, None] * inv[None, :].astype(np.float32)
    cos = np.cos(ang).astype(np.float32)
    sin = np.sin(ang).astype(np.float32)
    return (jnp.asarray(np.concatenate([cos, cos], axis=-1)),
            jnp.asarray(np.concatenate([-sin, sin], axis=-1)))


def _split_mod(mod, batch):
    m = mod[:batch].reshape(batch, 1, 3, D_MODEL)
    return m[:, :, 0, :], m[:, :, 1, :], m[:, :, 2, :]


def kernel(x, c, ab_norm_g, ab_w_mod, ab_b_mod, ab_w_in, ab_conv_w, ab_w_out, sg_norm_g, sg_w_mod, sg_b_mod, sg_w_in, sg_ln_g, sg_ln_b, sg_w_s, sg_b_s, sg_w_out, final_norm_g):
    batch, s, _ = x.shape
    depth = ab_w_in.shape[0] + sg_w_in.shape[0]
    c_pad = jnp.pad(c, ((0, BF16_ROWS - batch), (0, 0)))
    cos, sin = _rope_tables(s)

    w_in_b = ab_w_in[0].astype(BF16)
    sg_w_s_b = sg_w_s.astype(BF16)

    def layer_params(layer, mod_vec):
        norm_g = (ab_norm_g, sg_norm_g)[layer % 2][layer // 2]
        return (norm_g,) + _split_mod(mod_vec, batch)

    def mod_weights(layer):
        w_mod, b_mod = ((ab_w_mod, ab_b_mod), (sg_w_mod, sg_b_mod))[layer % 2]
        return w_mod, b_mod, layer // 2

    params = layer_params(0, _modulation(c_pad, *mod_weights(0)))

    h = None
    for layer in range(depth):
        i = layer // 2
        norm_g, shift, scale, gate = params
        final = layer == depth - 1
        mod = None if final else (c_pad,) + mod_weights(layer + 1)
        src, normalize = (x, True) if h is None else (h, False)
        if layer % 2 == 0:
            casts = ((ab_w_out, i),) + (() if final else ((sg_w_in, i),))
            proj, w_out_b, *more = _in_projection(src, norm_g, shift, scale, w_in_b, cos, sin,
                                                  epilogues=AB_EPILOGUES, normalize=normalize,
                                                  casts=casts, mod=mod)
        else:
            casts = ((sg_w_out, i),) + (() if final else ((ab_w_in, i + 1),))
            proj, w_out_b, *more = _in_projection(src, norm_g, shift, scale, w_in_b, cos, sin,
                                                  epilogues=SG_EPILOGUES, normalize=normalize,
                                                  casts=casts, mod=mod)
        if final:
            nxt = (final_norm_g, gate, gate)
        else:
            w_in_b, mod_vec = more
            params = layer_params(layer + 1, mod_vec)
            nxt = params[:3]
        if layer % 2 == 0:
            res = _ab_out_projection(_attention(proj), proj, ab_conv_w[i], w_out_b, x, gate, *nxt,
                                     final=final)
        else:
            res = _sgu(proj, sg_ln_g[i], sg_ln_b[i], sg_w_s_b[i], sg_b_s[i].T, w_out_b, x, gate,
                       *nxt, final=final)
        if final:
            return res
        x, h = res
```

```python
import functools

import jax
import jax.numpy as jnp
import numpy as np
from jax import lax
from jax.experimental import pallas as pl
from jax.experimental.pallas import tpu as pltpu

F32 = jnp.float32
BF16 = jnp.bfloat16

D_MODEL = 2048
HEAD_DIM = 128
A_WIDTH = 1024
A_HEADS = 8
B_WIDTH = 1024
B_CONV = 3
DILATIONS = (1, 4, 16)
RADIUS = 64
ROPE_THETA = 10000.0
NEG_INF = -1e30
C_WIDTH = 2048
C_GROUPS = 8
C_CHUNK = 128
EPS = 1e-6

LOG2_E = 1.4426950408889634
Q_SCALE = HEAD_DIM ** -0.5 * LOG2_E
Q_TILE = 128
K_WIN = Q_TILE + 2 * RADIUS
TILE_GROUP = 32
CAST_SLABS = 32
MOD_COLS = 128
BF16_ROWS = 16
LN_ROWS = 64
VMEM_LIMIT = 60 * 1024 * 1024

AB_EPILOGUES = (("rope_q", "rope_k"), ("none", "silu"), ("none", "none"), ("none", "silu"))
SG_EPILOGUES = (("gelu", "gelu"), ("gelu_ln",), ("silu", "silu"))


def _silu(z):
    hz = 0.5 * z
    return hz + hz * jnp.tanh(hz)


def _gelu_tanh(x):
    c = 0.7978845608028654
    hx = 0.5 * x
    return hx + hx * jnp.tanh(x * (c + (c * 0.044715) * (x * x)))


def _rms_norm(x, gain):
    ms = jnp.mean(x * x, axis=-1, keepdims=True)
    return x * lax.rsqrt(ms + EPS) * gain


def _modulated_rms_norm(x, gm, shift):
    return _rms_norm(x, gm) + shift


def _row_groups(tm):
    edges = (0, tm // 2, tm)
    return tuple(zip(edges[:-1], edges[1:]))


def _residual_tail(xn, rows, ng_ref, nshift_ref, nscale_ref, o_ref, h_ref, final):
    if final:
        o_ref[rows, :] = _rms_norm(xn, ng_ref[...])
    else:
        o_ref[rows, :] = xn
        gm = ng_ref[...] * (1.0 + nscale_ref[...])
        h_ref[rows, :] = _modulated_rms_norm(xn, gm, nshift_ref[...]).astype(BF16)


def _mod_kernel(c_ref, w_ref, b_ref, o_ref):
    a = _silu(c_ref[...]).astype(BF16)
    w = w_ref[...].astype(BF16)
    o_ref[...] = jnp.dot(a, w, preferred_element_type=F32) + b_ref[...]


def _modulation(c_pad, w_mod, b_mod, layer, tn=1024):
    _, d, n = w_mod.shape
    rows = c_pad.shape[0]
    return pl.pallas_call(
        _mod_kernel,
        grid=(n // tn,),
        in_specs=[
            pl.BlockSpec((rows, d), lambda j: (0, 0)),
            pl.BlockSpec((None, d, tn), lambda j: (layer, 0, j)),
            pl.BlockSpec((None, 1, tn), lambda j: (layer, 0, j)),
        ],
        out_specs=pl.BlockSpec((rows, tn), lambda j: (0, j)),
        out_shape=jax.ShapeDtypeStruct((rows, n), F32),
        compiler_params=pltpu.CompilerParams(
            dimension_semantics=("arbitrary",),
            vmem_limit_bytes=VMEM_LIMIT),
        name="modulation",
    )(c_pad, w_mod, b_mod.reshape(b_mod.shape[0], 1, n))


def _inproj_kernel(x_ref, g_ref, shift_ref, scale_ref, w_ref, cos_ref, sin_ref, *rest,
                   has_ln, n_casts, has_mod, epilogues, seg, normalize, norm_groups, row_chunk,
                   out_chunk):
    j = pl.program_id(2)
    tm = x_ref.shape[0]
    ln_src, rest = (rest[:2], rest[2:]) if has_ln else ((), rest)
    n_mod = 3 if has_mod else 0
    cast_src, mod_src = rest[:n_casts], rest[n_casts:n_casts + n_mod]
    o_ref, rest = rest[n_casts + n_mod], rest[n_casts + n_mod + 1:]
    n_dst = n_casts + n_mod // 3
    cast_dst, mod_dst, scratch = rest[:n_casts], rest[n_casts:n_dst], rest[n_dst:]
    h_ref = scratch[0] if normalize else x_ref

    def normalize_rows(r0, r1):
        gm = g_ref[...] * (1.0 + scale_ref[...])
        shift = shift_ref[...]
        for c0 in range(r0, r1, row_chunk):
            rows = slice(c0, c0 + row_chunk)
            h_ref[rows, :] = _modulated_rms_norm(x_ref[rows, :], gm, shift).astype(BF16)

    def rope(acc, r0, c0, f):
        rows = slice(r0, r0 + acc.shape[0])
        cos = cos_ref[rows, :] * f
        sin = sin_ref[rows, :] * f
        for hh in range(seg // HEAD_DIM):
            cols = slice(c0 + hh * HEAD_DIM, c0 + (hh + 1) * HEAD_DIM)
            blk = acc[:, cols]
            o_ref[rows, cols] = (blk * cos + pltpu.roll(blk, HEAD_DIM // 2, 1) * sin).astype(BF16)

    def pointwise(acc, r0, c0, fn):
        for q0 in range(0, acc.shape[0], out_chunk):
            o_ref[r0 + q0:r0 + q0 + out_chunk, c0:c0 + seg] = fn(
                acc[q0:q0 + out_chunk, c0:c0 + seg]).astype(BF16)

    def gelu_layer_norm(acc, r0, c0):
        gain, bias = ln_src[0][...], ln_src[1][...]
        for q0 in range(0, acc.shape[0], LN_ROWS):
            v = _gelu_tanh(acc[q0:q0 + LN_ROWS, :])
            vc = v - jnp.mean(v, axis=-1, keepdims=True)
            var = jnp.mean(vc * vc, axis=-1, keepdims=True)
            o_ref[r0 + q0:r0 + q0 + LN_ROWS, :] = (
                vc * lax.rsqrt(var + EPS) * gain + bias).astype(BF16)

    finish = {
        "rope_q": lambda acc, r0, c0: rope(acc, r0, c0, Q_SCALE),
        "rope_k": lambda acc, r0, c0: rope(acc, r0, c0, 1.0),
        "none": lambda acc, r0, c0: pointwise(acc, r0, c0, lambda t: t),
        "silu": lambda acc, r0, c0: pointwise(acc, r0, c0, _silu),
        "gelu": lambda acc, r0, c0: pointwise(acc, r0, c0, _gelu_tanh),
        "gelu_ln": gelu_layer_norm,
    }

    def project(kinds, groups, with_norm):
        for src, dst in zip(cast_src, cast_dst):
            dst[...] = src[...].astype(BF16)
        if has_mod:
            _mod_kernel(*mod_src, *mod_dst)
        for r0, r1 in groups:
            if with_norm:
                normalize_rows(r0, r1)
            acc = jnp.dot(h_ref[r0:r1, :], w_ref[...], preferred_element_type=F32)
            for si, kind in enumerate(kinds):
                finish[kind](acc, r0, si * seg)

    plain_tiles = list(enumerate(epilogues))
    if normalize:
        plain_tiles = plain_tiles[1:]

        @pl.when(j == 0)
        def _():
            project(epilogues[0], norm_groups, True)

    for kinds in sorted(set(e for _, e in plain_tiles)):
        tiles = [t for t, e in plain_tiles if e == kinds]
        cond = functools.reduce(jnp.logical_or, [j == t for t in tiles])

        @pl.when(cond)
        def _(kinds=kinds):
            project(kinds, norm_groups if "gelu_ln" in kinds else ((0, tm),), False)


def _in_projection(x, norm_g, shift, scale, w_bf16, cos, sin, *, epilogues, normalize, casts=(),
                   mod=None, ln=None, tm=1024):
    b, s, d = x.shape
    n = w_bf16.shape[1]
    tn = n // len(epilogues)
    seg = tn // len(epilogues[0])
    n_i, n_j = s // tm, n // tn
    n_steps = b * n_i * n_j
    assert n_steps >= CAST_SLABS

    def step(bi, i, j):
        return (bi * n_i + i) * n_j + j

    def slab(bi, i, j):
        return jnp.minimum(step(bi, i, j), CAST_SLABS - 1)

    cast_in, cast_out, cast_shape = [], [], []
    for stack, layer in casts:
        _, rows, cols = stack.shape
        cast_in.append(pl.BlockSpec((None, rows // CAST_SLABS, cols),
                                    lambda bi, i, j, layer=layer: (layer, slab(bi, i, j), 0)))
        cast_out.append(pl.BlockSpec((rows // CAST_SLABS, cols),
                                     lambda bi, i, j: (slab(bi, i, j), 0)))
        cast_shape.append(jax.ShapeDtypeStruct((rows, cols), BF16))
    mod_in, mod_out, mod_shape, mod_args = [], [], [], []
    if mod is not None:
        c_pad, w_mod, b_mod, mod_layer = mod
        n_mod = w_mod.shape[2]
        mod_slabs = n_mod // MOD_COLS
        assert n_steps >= mod_slabs

        def mod_slab(bi, i, j):
            return jnp.minimum(step(bi, i, j), mod_slabs - 1)

        mod_in = [pl.BlockSpec(c_pad.shape, lambda bi, i, j: (0, 0)),
                  pl.BlockSpec((None, d, MOD_COLS),
                               lambda bi, i, j: (mod_layer, 0, mod_slab(bi, i, j))),
                  pl.BlockSpec((None, 1, MOD_COLS),
                               lambda bi, i, j: (mod_layer, 0, mod_slab(bi, i, j)))]
        mod_out = [pl.BlockSpec((c_pad.shape[0], MOD_COLS),
                                lambda bi, i, j: (0, mod_slab(bi, i, j)))]
        mod_shape = [jax.ShapeDtypeStruct((c_pad.shape[0], n_mod), F32)]
        mod_args = [c_pad, w_mod, b_mod.reshape(b_mod.shape[0], 1, n_mod)]
    ln_in, ln_args = [], []
    if ln is not None:
        ln_in = [pl.BlockSpec((1, tn), lambda bi, i, j: (0, 0))] * 2
        ln_args = [v.reshape(1, tn) for v in ln]
    kern = functools.partial(_inproj_kernel, has_ln=ln is not None, n_casts=len(casts),
                             has_mod=mod is not None,
                             epilogues=epilogues, seg=seg,
                             normalize=normalize,
                             norm_groups=tuple((r, r + tm // 4) for r in range(0, tm, tm // 4)),
                             row_chunk=BF16_ROWS, out_chunk=256)
    return pl.pallas_call(
        kern,
        grid=(b, s // tm, n // tn),
        in_specs=[
            pl.BlockSpec((None, tm, d), lambda bi, i, j: (bi, i, 0)),
            pl.BlockSpec((1, d), lambda bi, i, j: (0, 0)),
            pl.BlockSpec((None, 1, d), lambda bi, i, j: (bi, 0, 0)),
            pl.BlockSpec((None, 1, d), lambda bi, i, j: (bi, 0, 0)),
            pl.BlockSpec((d, tn), lambda bi, i, j: (0, j)),
            pl.BlockSpec((tm, HEAD_DIM), lambda bi, i, j: (i, 0)),
            pl.BlockSpec((tm, HEAD_DIM), lambda bi, i, j: (i, 0)),
        ] + ln_in + cast_in + mod_in,
        out_specs=[pl.BlockSpec((None, tm, tn), lambda bi, i, j: (bi, i, j))] + cast_out + mod_out,
        out_shape=[jax.ShapeDtypeStruct((b, s, n), BF16)] + cast_shape + mod_shape,
        scratch_shapes=[pltpu.VMEM((tm, d), BF16)] if normalize else [],
        compiler_params=pltpu.CompilerParams(
            dimension_semantics=("arbitrary", "arbitrary", "arbitrary"),
            vmem_limit_bytes=VMEM_LIMIT),
        name="in_projection_ab" if "rope_q" in epilogues[0] else "in_projection_sg",
    )(x, norm_g.reshape(1, d), shift, scale, w_bf16, cos, sin, *ln_args,
      *[stack for stack, _ in casts], *mod_args)


def _attention_kernel(q_ref, k_ref, v_ref, za_ref, ya_ref,
                      fa, fb, q4, k4, v4, q16, k16, v16, v1, mask_scr):
    s = q_ref.shape[0]
    d4, d16 = DILATIONS[1], DILATIONS[2]
    n4, n16 = s // d4, s // d16
    step = d16 // d4
    chunk = 256

    qi = lax.broadcasted_iota(jnp.int32, (Q_TILE, K_WIN), 0)
    ki = lax.broadcasted_iota(jnp.int32, (Q_TILE, K_WIN), 1)
    for t in range(3):
        mask_scr[t] = jnp.where(jnp.abs(qi + t * RADIUS - ki) <= RADIUS, 0.0, NEG_INF)

    ones = jnp.ones((s, HEAD_DIM), BF16)
    v1[:, HEAD_DIM:2 * HEAD_DIM] = ones
    v4[:, HEAD_DIM:2 * HEAD_DIM] = ones
    v16[:, HEAD_DIM:2 * HEAD_DIM] = ones
    v1[:, 0:HEAD_DIM] = v_ref[...]

    for idx, (src, dst4, dst16) in enumerate(((q_ref, q4, q16), (k_ref, k4, k16),
                                              (v_ref, v4, v16))):
        for c0 in range(0, s, chunk):
            fa[idx, c0:c0 + chunk, :] = src[c0:c0 + chunk, :].astype(F32)
        for r in range(d4):
            for c0 in range(0, n4, chunk):
                part = fa[idx, pl.ds(r + d4 * c0, chunk, stride=d4), :]
                fb[idx, r * n4 + c0:r * n4 + c0 + chunk, :] = part
                dst4[r * n4 + c0:r * n4 + c0 + chunk, 0:HEAD_DIM] = part.astype(BF16)
        for r in range(d4):
            for a in range(step):
                part = fb[idx, pl.ds(r * n4 + a, n16, stride=step), :]
                r16 = r + d4 * a
                dst16[r16 * n16:(r16 + 1) * n16, 0:HEAD_DIM] = part.astype(BF16)

    patterns = ((d16, q16, k16, v16), (d4, q4, k4, v4), (1, q_ref, k_ref, v1))
    states = (None, fa, fb)
    for p, (dil, qs, ks, vs) in enumerate(patterns):
        n = s // dil
        tiles_per_seg = n // Q_TILE

        def one_tile(t, qs=qs, ks=ks, vs=vs, dil=dil, n=n, tiles_per_seg=tiles_per_seg, p=p):
            seg = t // tiles_per_seg
            l0 = (t % tiles_per_seg) * Q_TILE
            kstart = jnp.clip(l0 - RADIUS, 0, n - K_WIN)
            which = (l0 - kstart) // RADIUS
            row0 = pl.multiple_of(seg * n + l0, Q_TILE)
            krow0 = pl.multiple_of(seg * n + kstart, RADIUS)
            rows = pl.ds(row0, Q_TILE)
            qt = qs[rows, :]
            kt = ks[pl.ds(krow0, K_WIN), :]
            vt = vs[pl.ds(krow0, K_WIN), :]
            sc = lax.dot_general(qt, kt, (((1,), (1,)), ((), ())),
                                 preferred_element_type=F32)
            sc = sc + mask_scr[which]
            m = jnp.max(sc, axis=-1, keepdims=True)
            e = jnp.exp2(sc - m).astype(BF16)
            ov = jnp.dot(e, vt, preferred_element_type=F32)
            num = ov[:, 0:HEAD_DIM]
            den = ov[:, HEAD_DIM:2 * HEAD_DIM]
            m = jnp.broadcast_to(m, (Q_TILE, HEAD_DIM))
            if p > 0:
                prev = states[p]
                m_prev = prev[1, rows, :]
                m_all = jnp.maximum(m_prev, m)
                w_prev = jnp.exp2(m_prev - m_all)
                w_cur = jnp.exp2(m - m_all)
                num = w_prev * prev[0, rows, :] + w_cur * num
                den = w_prev * prev[2, rows, :] + w_cur * den
                m = m_all
            if p + 1 < len(patterns):
                nxt, dil_next = states[p + 1], patterns[p + 1][0]
                dst0 = (seg % dil_next) * (s // dil_next) + (dil // dil_next) * l0 + seg // dil_next
                dst = pl.ds(dst0, Q_TILE, stride=dil // dil_next)
                nxt[0, dst, :] = num
                nxt[1, dst, :] = m
                nxt[2, dst, :] = den
            else:
                ya_ref[rows, :] = (num / den * za_ref[rows, :].astype(F32)).astype(BF16)

        def group_body(gi, carry, one_tile=one_tile):
            for u in range(TILE_GROUP):
                one_tile(gi * TILE_GROUP + u)
            return carry

        lax.fori_loop(0, s // (Q_TILE * TILE_GROUP), group_body, 0)


def _attention(proj):
    b, s, _ = proj.shape
    nblk = A_WIDTH // HEAD_DIM

    def col(slot):
        return pl.BlockSpec((None, s, HEAD_DIM), lambda bi, h: (bi, 0, slot * nblk + h))

    return pl.pallas_call(
        _attention_kernel,
        grid=(b, A_HEADS),
        in_specs=[col(0), col(1), col(2), col(3)],
        out_specs=pl.BlockSpec((None, s, HEAD_DIM), lambda bi, h: (bi, 0, h)),
        out_shape=jax.ShapeDtypeStruct((b, s, A_WIDTH), BF16),
        scratch_shapes=[
            pltpu.VMEM((3, s, HEAD_DIM), F32), pltpu.VMEM((3, s, HEAD_DIM), F32),
            pltpu.VMEM((s, HEAD_DIM), BF16), pltpu.VMEM((s, HEAD_DIM), BF16),
            pltpu.VMEM((s, 2 * HEAD_DIM), BF16),
            pltpu.VMEM((s, HEAD_DIM), BF16), pltpu.VMEM((s, HEAD_DIM), BF16),
            pltpu.VMEM((s, 2 * HEAD_DIM), BF16),
            pltpu.VMEM((s, 2 * HEAD_DIM), BF16),
            pltpu.VMEM((3, Q_TILE, K_WIN), F32),
        ],
        compiler_params=pltpu.CompilerParams(
            dimension_semantics=("arbitrary", "arbitrary"),
            vmem_limit_bytes=VMEM_LIMIT),
        name="attention",
    )(proj, proj, proj, proj)


def _ab_out_kernel(ya_ref, ub_ref, gb_ref, gc_ref, zb_ref, ubp_ref, gcp_ref, ubn_ref, gcn_ref,
                   cw_ref, w_ref, x_ref, gate_ref, ng_ref, nshift_ref, nscale_ref,
                   o_ref, *rest, final, row_groups):
    h_ref, pbuf, yb_scr = ((None,) + rest) if final else rest
    i = pl.program_id(1)
    tm = x_ref.shape[0]
    halo = ubp_ref.shape[0]
    chunk = 32

    prev = (gcp_ref[...].astype(F32) * ubp_ref[...].astype(F32))[halo - 1:halo, :]
    nxt = (gcn_ref[...].astype(F32) * ubn_ref[...].astype(F32))[0:1, :]
    pbuf[7:8, :] = jnp.where(i > 0, prev, 0.0)
    pbuf[8 + tm:9 + tm, :] = jnp.where(i < pl.num_programs(1) - 1, nxt, 0.0)
    for c0 in range(0, tm, chunk):
        rows = slice(c0, c0 + chunk)
        pbuf[8 + c0:8 + c0 + chunk, :] = gc_ref[rows, :].astype(F32) * ub_ref[rows, :].astype(F32)
    w0 = cw_ref[0:1, :]
    w1 = cw_ref[1:2, :]
    w2 = cw_ref[2:3, :]

    for d0, d1 in row_groups:
        for c0 in range(d0, d1, chunk):
            rows = slice(c0, c0 + chunk)
            conv = (w0 * pbuf[7 + c0:7 + c0 + chunk, :] + w1 * pbuf[8 + c0:8 + c0 + chunk, :]
                    + w2 * pbuf[9 + c0:9 + c0 + chunk, :])
            gate = gb_ref[rows, :].astype(F32) * zb_ref[rows, :].astype(F32)
            yb_scr[rows, :] = (gate * conv).astype(BF16)
        rows = slice(d0, d1)
        out = jnp.dot(ya_ref[rows, :], w_ref[0:A_WIDTH, :], preferred_element_type=F32)
        out = out + jnp.dot(yb_scr[rows, :], w_ref[A_WIDTH:A_WIDTH + B_WIDTH, :],
                            preferred_element_type=F32)
        xn = x_ref[rows, :] + gate_ref[...] * out
        _residual_tail(xn, rows, ng_ref, nshift_ref, nscale_ref, o_ref, h_ref, final)


def _residual_out(b, s, d, tm, final):
    spec = pl.BlockSpec((None, tm, d), lambda bi, i: (bi, i, 0))
    if final:
        return spec, jax.ShapeDtypeStruct((b, s, d), F32)
    return [spec, spec], [jax.ShapeDtypeStruct((b, s, d), F32),
                          jax.ShapeDtypeStruct((b, s, d), BF16)]


def _ab_out_projection(ya, proj, conv_w, w_bf16, x, gate, next_g, next_shift, next_scale, *,
                       final, tm=512, halo=BF16_ROWS):
    b, s, d = x.shape
    out_specs, out_shape = _residual_out(b, s, d, tm, final)
    row_vec = pl.BlockSpec((None, 1, d), lambda bi, i: (bi, 0, 0))
    per_tile = tm // halo
    n_halo = s // halo

    def slot(k):
        return pl.BlockSpec((None, tm, B_WIDTH), lambda bi, i: (bi, i, k))

    def before(k):
        return pl.BlockSpec((None, halo, B_WIDTH),
                            lambda bi, i: (bi, jnp.maximum(i * per_tile - 1, 0), k))

    def after(k):
        return pl.BlockSpec((None, halo, B_WIDTH),
                            lambda bi, i: (bi, jnp.minimum((i + 1) * per_tile, n_halo - 1), k))

    return pl.pallas_call(
        functools.partial(_ab_out_kernel, final=final, row_groups=_row_groups(tm)),
        grid=(b, s // tm),
        in_specs=[
            pl.BlockSpec((None, tm, A_WIDTH), lambda bi, i: (bi, i, 0)),
            slot(4), slot(5), slot(6), slot(7),
            before(4), before(6), after(4), after(6),
            pl.BlockSpec((B_CONV, B_WIDTH), lambda bi, i: (0, 0)),
            pl.BlockSpec((A_WIDTH + B_WIDTH, d), lambda bi, i: (0, 0),
                         pipeline_mode=pl.Buffered(1)),
            pl.BlockSpec((None, tm, d), lambda bi, i: (bi, i, 0)),
            row_vec,
            pl.BlockSpec((1, d), lambda bi, i: (0, 0)),
            row_vec,
            row_vec,
        ],
        out_specs=out_specs,
        out_shape=out_shape,
        scratch_shapes=[pltpu.VMEM((tm + 16, B_WIDTH), F32), pltpu.VMEM((tm, B_WIDTH), BF16)],
        compiler_params=pltpu.CompilerParams(
            dimension_semantics=("arbitrary", "arbitrary"),
            vmem_limit_bytes=VMEM_LIMIT),
        name="ab_out_projection",
    )(ya, proj, proj, proj, proj, proj, proj, proj, proj, conv_w, w_bf16, x, gate,
      next_g.reshape(1, d), next_shift, next_scale)


def _sgu_kernel(u_ref, vn_ref, z_ref, ws_ref, bs_ref, w_ref, x_ref, gate_ref,
                ng_ref, nshift_ref, nscale_ref, o_ref, *rest, final, row_groups):
    h_ref, y_scr = (None, rest[0]) if final else rest
    tm = u_ref.shape[0]
    gw = C_WIDTH // C_GROUPS
    for d0, d1 in row_groups:
        for c0 in range(d0, d1, C_CHUNK):
            rows = slice(c0, c0 + C_CHUNK)
            vn = vn_ref[rows, :]
            for g in range(C_GROUPS):
                cols = slice(g * gw, (g + 1) * gw)
                mixed = jnp.dot(ws_ref[g], vn[:, cols], preferred_element_type=F32)
                mixed = mixed + bs_ref[:, g:g + 1]
                gated = u_ref[rows, cols].astype(F32) * z_ref[rows, cols].astype(F32)
                y_scr[rows, cols] = (gated * mixed).astype(BF16)
        rows = slice(d0, d1)
        out = jnp.dot(y_scr[rows, :], w_ref[...], preferred_element_type=F32)
        xn = x_ref[rows, :] + gate_ref[...] * out
        _residual_tail(xn, rows, ng_ref, nshift_ref, nscale_ref, o_ref, h_ref, final)


def _sgu(proj, ws_bf16, bs_t, w_bf16, x, gate, next_g, next_shift, next_scale, *,
         final, tm=512):
    b, s, d = x.shape
    cw = C_WIDTH
    kern = functools.partial(_sgu_kernel, final=final, row_groups=_row_groups(tm))
    out_specs, out_shape = _residual_out(b, s, d, tm, final)
    row_vec = pl.BlockSpec((None, 1, d), lambda bi, i: (bi, 0, 0))
    return pl.pallas_call(
        kern,
        grid=(b, s // tm),
        in_specs=[
            pl.BlockSpec((None, tm, cw), lambda bi, i: (bi, i, 0)),
            pl.BlockSpec((None, tm, cw), lambda bi, i: (bi, i, 1)),
            pl.BlockSpec((None, tm, cw), lambda bi, i: (bi, i, 2)),
            pl.BlockSpec((C_GROUPS, C_CHUNK, C_CHUNK), lambda bi, i: (0, 0, 0)),
            pl.BlockSpec((C_CHUNK, C_GROUPS), lambda bi, i: (0, 0)),
            pl.BlockSpec((cw, d), lambda bi, i: (0, 0), pipeline_mode=pl.Buffered(1)),
            pl.BlockSpec((None, tm, d), lambda bi, i: (bi, i, 0)),
            row_vec,
            pl.BlockSpec((1, d), lambda bi, i: (0, 0)),
            row_vec,
            row_vec,
        ],
        out_specs=out_specs,
        out_shape=out_shape,
        scratch_shapes=[pltpu.VMEM((tm, cw), BF16)],
        compiler_params=pltpu.CompilerParams(
            dimension_semantics=("arbitrary", "arbitrary"),
            vmem_limit_bytes=VMEM_LIMIT),
        name="sgu_final" if final else "sgu",
    )(proj, proj, proj, ws_bf16, bs_t, w_bf16, x, gate, next_g.reshape(1, d), next_shift,
      next_scale)


def _rope_tables(s):
    half = HEAD_DIM // 2
    inv = np.float32(ROPE_THETA) ** (-np.arange(half, dtype=np.float32) / np.float32(half))
    ang = np.arange(s, dtype=np.float32)[:, None] * inv[None, :].astype(np.float32)
    cos = np.cos(ang).astype(np.float32)
    sin = np.sin(ang).astype(np.float32)
    return (jnp.asarray(np.concatenate([cos, cos], axis=-1)),
            jnp.asarray(np.concatenate([-sin, sin], axis=-1)))


def _split_mod(mod, batch):
    m = mod[:batch].reshape(batch, 1, 3, D_MODEL)
    return m[:, :, 0, :], m[:, :, 1, :], m[:, :, 2, :]


def kernel(x, c, ab_norm_g, ab_w_mod, ab_b_mod, ab_w_in, ab_conv_w, ab_w_out, sg_norm_g, sg_w_mod, sg_b_mod, sg_w_in, sg_ln_g, sg_ln_b, sg_w_s, sg_b_s, sg_w_out, final_norm_g):
    batch, s, _ = x.shape
    depth = ab_w_in.shape[0] + sg_w_in.shape[0]
    c_pad = jnp.pad(c, ((0, BF16_ROWS - batch), (0, 0)))
    cos, sin = _rope_tables(s)

    w_in_b = ab_w_in[0].astype(BF16)
    sg_w_s_b = sg_w_s.astype(BF16)

    def layer_params(layer, mod_vec):
        norm_g = (ab_norm_g, sg_norm_g)[layer % 2][layer // 2]
        return (norm_g,) + _split_mod(mod_vec, batch)

    def mod_weights(layer):
        w_mod, b_mod = ((ab_w_mod, ab_b_mod), (sg_w_mod, sg_b_mod))[layer % 2]
        return w_mod, b_mod, layer // 2

    params = layer_params(0, _modulation(c_pad, *mod_weights(0)))

    h = None
    for layer in range(depth):
        i = layer // 2
        norm_g, shift, scale, gate = params
        final = layer == depth - 1
        mod = None if final else (c_pad,) + mod_weights(layer + 1)
        src, normalize = (x, True) if h is None else (h, False)
        if layer % 2 == 0:
            casts = ((ab_w_out, i),) + (() if final else ((sg_w_in, i),))
            proj, w_out_b, *more = _in_projection(src, norm_g, shift, scale, w_in_b, cos, sin,
                                                  epilogues=AB_EPILOGUES, normalize=normalize,
                                                  casts=casts, mod=mod)
        else:
            casts = ((sg_w_out, i),) + (() if final else ((ab_w_in, i + 1),))
            proj, w_out_b, *more = _in_projection(src, norm_g, shift, scale, w_in_b, cos, sin,
                                                  epilogues=SG_EPILOGUES, normalize=normalize,
                                                  casts=casts, mod=mod,
                                                  ln=(sg_ln_g[i], sg_ln_b[i]))
        if final:
            nxt = (final_norm_g, gate, gate)
        else:
            w_in_b, mod_vec = more
            params = layer_params(layer + 1, mod_vec)
            nxt = params[:3]
        if layer % 2 == 0:
            res = _ab_out_projection(_attention(proj), proj, ab_conv_w[i], w_out_b, x, gate, *nxt,
                                     final=final)
        else:
            res = _sgu(proj, sg_w_s_b[i], sg_b_s[i].T, w_out_b, x, gate, *nxt, final=final)
        if final:
            return res
        x, h = res
```

```python
import functools

import jax
import jax.numpy as jnp
import numpy as np
from jax import lax
from jax.experimental import pallas as pl
from jax.experimental.pallas import tpu as pltpu

F32 = jnp.float32
BF16 = jnp.bfloat16

D_MODEL = 2048
HEAD_DIM = 128
A_WIDTH = 1024
A_HEADS = 8
B_WIDTH = 1024
B_CONV = 3
DILATIONS = (1, 4, 16)
RADIUS = 64
ROPE_THETA = 10000.0
NEG_INF = -1e30
C_WIDTH = 2048
C_GROUPS = 8
C_CHUNK = 128
EPS = 1e-6

LOG2_E = 1.4426950408889634
Q_SCALE = HEAD_DIM ** -0.5 * LOG2_E
Q_TILE = 128
K_WIN = Q_TILE + 2 * RADIUS
TILE_GROUP = 32
CAST_SLABS = 32
MOD_COLS = 128
BF16_ROWS = 16
LN_ROWS = 64
VMEM_LIMIT = 60 * 1024 * 1024

AB_EPILOGUES = (("rope_q", "rope_k"), ("none", "silu"), ("none", "none"), ("none", "silu"))
SG_EPILOGUES = (("gelu", "gelu"), ("gelu_ln",), ("silu", "silu"))


def _silu(z):
    hz = 0.5 * z
    return hz + hz * jnp.tanh(hz)


def _gelu_tanh(x):
    c = 0.7978845608028654
    hx = 0.5 * x
    return hx + hx * jnp.tanh(x * (c + (c * 0.044715) * (x * x)))


def _rms_norm(x, gain):
    ms = jnp.mean(x * x, axis=-1, keepdims=True)
    return x * lax.rsqrt(ms + EPS) * gain


def _modulated_rms_norm(x, gm, shift):
    return _rms_norm(x, gm) + shift


def _row_groups(tm):
    edges = (0, tm // 2, tm)
    return tuple(zip(edges[:-1], edges[1:]))


def _residual_tail(xn, rows, ng_ref, nshift_ref, nscale_ref, o_ref, h_ref, final):
    if final:
        o_ref[rows, :] = _rms_norm(xn, ng_ref[...])
    else:
        o_ref[rows, :] = xn
        gm = ng_ref[...] * (1.0 + nscale_ref[...])
        h_ref[rows, :] = _modulated_rms_norm(xn, gm, nshift_ref[...]).astype(BF16)


def _mod_kernel(c_ref, w_ref, b_ref, o_ref):
    a = _silu(c_ref[...]).astype(BF16)
    w = w_ref[...].astype(BF16)
    o_ref[...] = jnp.dot(a, w, preferred_element_type=F32) + b_ref[...]


def _modulation(c_pad, w_mod, b_mod, layer, tn=1024):
    _, d, n = w_mod.shape
    rows = c_pad.shape[0]
    return pl.pallas_call(
        _mod_kernel,
        grid=(n // tn,),
        in_specs=[
            pl.BlockSpec((rows, d), lambda j: (0, 0)),
            pl.BlockSpec((None, d, tn), lambda j: (layer, 0, j)),
            pl.BlockSpec((None, 1, tn), lambda j: (layer, 0, j)),
        ],
        out_specs=pl.BlockSpec((rows, tn), lambda j: (0, j)),
        out_shape=jax.ShapeDtypeStruct((rows, n), F32),
        compiler_params=pltpu.CompilerParams(
            dimension_semantics=("arbitrary",),
            vmem_limit_bytes=VMEM_LIMIT),
        name="modulation",
    )(c_pad, w_mod, b_mod.reshape(b_mod.shape[0], 1, n))


def _inproj_kernel(x_ref, g_ref, shift_ref, scale_ref, w_ref, cos_ref, sin_ref, *rest,
                   has_ln, n_casts, has_mod, epilogues, seg, normalize, norm_groups, row_chunk,
                   out_chunk):
    j = pl.program_id(2)
    tm = x_ref.shape[0]
    ln_src, rest = (rest[:2], rest[2:]) if has_ln else ((), rest)
    n_mod = 3 if has_mod else 0
    cast_src, mod_src = rest[:n_casts], rest[n_casts:n_casts + n_mod]
    o_ref, rest = rest[n_casts + n_mod], rest[n_casts + n_mod + 1:]
    n_dst = n_casts + n_mod // 3
    cast_dst, mod_dst, scratch = rest[:n_casts], rest[n_casts:n_dst], rest[n_dst:]
    h_ref = scratch[0] if normalize else x_ref

    def normalize_rows(r0, r1):
        gm = g_ref[...] * (1.0 + scale_ref[...])
        shift = shift_ref[...]
        for c0 in range(r0, r1, row_chunk):
            rows = slice(c0, c0 + row_chunk)
            h_ref[rows, :] = _modulated_rms_norm(x_ref[rows, :], gm, shift).astype(BF16)

    def rope(acc, r0, c0, f):
        rows = slice(r0, r0 + acc.shape[0])
        cos = cos_ref[rows, :] * f
        sin = sin_ref[rows, :] * f
        for hh in range(seg // HEAD_DIM):
            cols = slice(c0 + hh * HEAD_DIM, c0 + (hh + 1) * HEAD_DIM)
            blk = acc[:, cols]
            o_ref[rows, cols] = (blk * cos + pltpu.roll(blk, HEAD_DIM // 2, 1) * sin).astype(BF16)

    def pointwise(acc, r0, c0, fn):
        for q0 in range(0, acc.shape[0], out_chunk):
            o_ref[r0 + q0:r0 + q0 + out_chunk, c0:c0 + seg] = fn(
                acc[q0:q0 + out_chunk, c0:c0 + seg]).astype(BF16)

    def gelu_layer_norm(acc, r0, c0):
        gain, bias = ln_src[0][...], ln_src[1][...]
        for q0 in range(0, acc.shape[0], LN_ROWS):
            v = _gelu_tanh(acc[q0:q0 + LN_ROWS, :])
            vc = v - jnp.mean(v, axis=-1, keepdims=True)
            var = jnp.mean(vc * vc, axis=-1, keepdims=True)
            o_ref[r0 + q0:r0 + q0 + LN_ROWS, :] = (
                vc * lax.rsqrt(var + EPS) * gain + bias).astype(BF16)

    finish = {
        "rope_q": lambda acc, r0, c0: rope(acc, r0, c0, Q_SCALE),
        "rope_k": lambda acc, r0, c0: rope(acc, r0, c0, 1.0),
        "none": lambda acc, r0, c0: pointwise(acc, r0, c0, lambda t: t),
        "silu": lambda acc, r0, c0: pointwise(acc, r0, c0, _silu),
        "gelu": lambda acc, r0, c0: pointwise(acc, r0, c0, _gelu_tanh),
        "gelu_ln": gelu_layer_norm,
    }

    def project(kinds, groups, with_norm):
        for src, dst in zip(cast_src, cast_dst):
            dst[...] = src[...].astype(BF16)
        if has_mod:
            _mod_kernel(*mod_src, *mod_dst)
        for r0, r1 in groups:
            if with_norm:
                normalize_rows(r0, r1)
            acc = jnp.dot(h_ref[r0:r1, :], w_ref[...], preferred_element_type=F32)
            for si, kind in enumerate(kinds):
                finish[kind](acc, r0, si * seg)

    plain_tiles = list(enumerate(epilogues))
    if normalize:
        plain_tiles = plain_tiles[1:]

        @pl.when(j == 0)
        def _():
            project(epilogues[0], norm_groups, True)

    for kinds in sorted(set(e for _, e in plain_tiles)):
        tiles = [t for t, e in plain_tiles if e == kinds]
        cond = functools.reduce(jnp.logical_or, [j == t for t in tiles])

        @pl.when(cond)
        def _(kinds=kinds):
            project(kinds, norm_groups if "gelu_ln" in kinds else ((0, tm),), False)


def _in_projection(x, norm_g, shift, scale, w_bf16, cos, sin, *, epilogues, normalize, casts=(),
                   mod=None, ln=None, tm=1024):
    b, s, d = x.shape
    n = w_bf16.shape[1]
    tn = n // len(epilogues)
    seg = tn // len(epilogues[0])
    n_i, n_j = s // tm, n // tn
    n_steps = b * n_i * n_j
    assert n_steps >= CAST_SLABS

    def step(bi, i, j):
        return (bi * n_i + i) * n_j + j

    def slab(bi, i, j):
        return jnp.minimum(step(bi, i, j), CAST_SLABS - 1)

    cast_in, cast_out, cast_shape = [], [], []
    for stack, layer in casts:
        _, rows, cols = stack.shape
        cast_in.append(pl.BlockSpec((None, rows // CAST_SLABS, cols),
                                    lambda bi, i, j, layer=layer: (layer, slab(bi, i, j), 0)))
        cast_out.append(pl.BlockSpec((rows // CAST_SLABS, cols),
                                     lambda bi, i, j: (slab(bi, i, j), 0)))
        cast_shape.append(jax.ShapeDtypeStruct((rows, cols), BF16))
    mod_in, mod_out, mod_shape, mod_args = [], [], [], []
    if mod is not None:
        c_pad, w_mod, b_mod, mod_layer = mod
        n_mod = w_mod.shape[2]
        mod_slabs = n_mod // MOD_COLS
        assert n_steps >= mod_slabs

        def mod_slab(bi, i, j):
            return jnp.minimum(step(bi, i, j), mod_slabs - 1)

        mod_in = [pl.BlockSpec(c_pad.shape, lambda bi, i, j: (0, 0)),
                  pl.BlockSpec((None, d, MOD_COLS),
                               lambda bi, i, j: (mod_layer, 0, mod_slab(bi, i, j))),
                  pl.BlockSpec((None, 1, MOD_COLS),
                               lambda bi, i, j: (mod_layer, 0, mod_slab(bi, i, j)))]
        mod_out = [pl.BlockSpec((c_pad.shape[0], MOD_COLS),
                                lambda bi, i, j: (0, mod_slab(bi, i, j)))]
        mod_shape = [jax.ShapeDtypeStruct((c_pad.shape[0], n_mod), F32)]
        mod_args = [c_pad, w_mod, b_mod.reshape(b_mod.shape[0], 1, n_mod)]
    ln_in, ln_args = [], []
    if ln is not None:
        ln_in = [pl.BlockSpec((1, tn), lambda bi, i, j: (0, 0))] * 2
        ln_args = [v.reshape(1, tn) for v in ln]
    kern = functools.partial(_inproj_kernel, has_ln=ln is not None, n_casts=len(casts),
                             has_mod=mod is not None,
                             epilogues=epilogues, seg=seg,
                             normalize=normalize,
                             norm_groups=tuple((r, r + tm // 4) for r in range(0, tm, tm // 4)),
                             row_chunk=BF16_ROWS, out_chunk=256)
    return pl.pallas_call(
        kern,
        grid=(b, s // tm, n // tn),
        in_specs=[
            pl.BlockSpec((None, tm, d), lambda bi, i, j: (bi, i, 0)),
            pl.BlockSpec((1, d), lambda bi, i, j: (0, 0)),
            pl.BlockSpec((None, 1, d), lambda bi, i, j: (bi, 0, 0)),
            pl.BlockSpec((None, 1, d), lambda bi, i, j: (bi, 0, 0)),
            pl.BlockSpec((d, tn), lambda bi, i, j: (0, j)),
            pl.BlockSpec((tm, HEAD_DIM), lambda bi, i, j: (i, 0)),
            pl.BlockSpec((tm, HEAD_DIM), lambda bi, i, j: (i, 0)),
        ] + ln_in + cast_in + mod_in,
        out_specs=[pl.BlockSpec((None, tm, tn), lambda bi, i, j: (bi, i, j))] + cast_out + mod_out,
        out_shape=[jax.ShapeDtypeStruct((b, s, n), BF16)] + cast_shape + mod_shape,
        scratch_shapes=[pltpu.VMEM((tm, d), BF16)] if normalize else [],
        compiler_params=pltpu.CompilerParams(
            dimension_semantics=("arbitrary", "arbitrary", "arbitrary"),
            vmem_limit_bytes=VMEM_LIMIT),
        name="in_projection_ab" if "rope_q" in epilogues[0] else "in_projection_sg",
    )(x, norm_g.reshape(1, d), shift, scale, w_bf16, cos, sin, *ln_args,
      *[stack for stack, _ in casts], *mod_args)


def _attention_kernel(q_ref, k_ref, v_ref, za_ref, ya_ref,
                      fa, fb, q4, k4, v4, q16, k16, v16, v1, mask_scr):
    s = q_ref.shape[0]
    d4, d16 = DILATIONS[1], DILATIONS[2]
    n4, n16 = s // d4, s // d16
    step = d16 // d4
    chunk = 256

    qi = lax.broadcasted_iota(jnp.int32, (Q_TILE, K_WIN), 0)
    ki = lax.broadcasted_iota(jnp.int32, (Q_TILE, K_WIN), 1)
    for t in range(3):
        mask_scr[t] = jnp.where(jnp.abs(qi + t * RADIUS - ki) <= RADIUS, 0.0, NEG_INF)

    ones = jnp.ones((s, HEAD_DIM), BF16)
    v1[:, HEAD_DIM:2 * HEAD_DIM] = ones
    v4[:, HEAD_DIM:2 * HEAD_DIM] = ones
    v16[:, HEAD_DIM:2 * HEAD_DIM] = ones
    v1[:, 0:HEAD_DIM] = v_ref[...]

    for idx, (src, dst4, dst16) in enumerate(((q_ref, q4, q16), (k_ref, k4, k16),
                                              (v_ref, v4, v16))):
        for c0 in range(0, s, chunk):
            fa[idx, c0:c0 + chunk, :] = src[c0:c0 + chunk, :].astype(F32)
        for r in range(d4):
            for c0 in range(0, n4, chunk):
                part = fa[idx, pl.ds(r + d4 * c0, chunk, stride=d4), :]
                fb[idx, r * n4 + c0:r * n4 + c0 + chunk, :] = part
                dst4[r * n4 + c0:r * n4 + c0 + chunk, 0:HEAD_DIM] = part.astype(BF16)
        for r in range(d4):
            for a in range(step):
                part = fb[idx, pl.ds(r * n4 + a, n16, stride=step), :]
                r16 = r + d4 * a
                dst16[r16 * n16:(r16 + 1) * n16, 0:HEAD_DIM] = part.astype(BF16)

    patterns = ((d16, q16, k16, v16), (d4, q4, k4, v4), (1, q_ref, k_ref, v1))
    states = (None, fa, fb)
    for p, (dil, qs, ks, vs) in enumerate(patterns):
        n = s // dil
        tiles_per_seg = n // Q_TILE

        def one_tile(t, qs=qs, ks=ks, vs=vs, dil=dil, n=n, tiles_per_seg=tiles_per_seg, p=p):
            seg = t // tiles_per_seg
            l0 = (t % tiles_per_seg) * Q_TILE
            kstart = jnp.clip(l0 - RADIUS, 0, n - K_WIN)
            which = (l0 - kstart) // RADIUS
            row0 = pl.multiple_of(seg * n + l0, Q_TILE)
            krow0 = pl.multiple_of(seg * n + kstart, RADIUS)
            rows = pl.ds(row0, Q_TILE)
            qt = qs[rows, :]
            kt = ks[pl.ds(krow0, K_WIN), :]
            vt = vs[pl.ds(krow0, K_WIN), :]
            sc = lax.dot_general(qt, kt, (((1,), (1,)), ((), ())),
                                 preferred_element_type=F32)
            sc = sc + mask_scr[which]
            m = jnp.max(sc, axis=-1, keepdims=True)
            e = jnp.exp2(sc - m).astype(BF16)
            ov = jnp.dot(e, vt, preferred_element_type=F32)
            num = ov[:, 0:HEAD_DIM]
            den = ov[:, HEAD_DIM:2 * HEAD_DIM]
            m = jnp.broadcast_to(m, (Q_TILE, HEAD_DIM))
            if p > 0:
                prev = states[p]
                m_prev = prev[1, rows, :]
                m_all = jnp.maximum(m_prev, m)
                w_prev = jnp.exp2(m_prev - m_all)
                w_cur = jnp.exp2(m - m_all)
                num = w_prev * prev[0, rows, :] + w_cur * num
                den = w_prev * prev[2, rows, :] + w_cur * den
                m = m_all
            if p + 1 < len(patterns):
                nxt, dil_next = states[p + 1], patterns[p + 1][0]
                dst0 = (seg % dil_next) * (s // dil_next) + (dil // dil_next) * l0 + seg // dil_next
                dst = pl.ds(dst0, Q_TILE, stride=dil // dil_next)
                nxt[0, dst, :] = num
                nxt[1, dst, :] = m
                nxt[2, dst, :] = den
            else:
                ya_ref[rows, :] = (num / den * za_ref[rows, :].astype(F32)).astype(BF16)

        def group_body(gi, carry, one_tile=one_tile):
            for u in range(TILE_GROUP):
                one_tile(gi * TILE_GROUP + u)
            return carry

        lax.fori_loop(0, s // (Q_TILE * TILE_GROUP), group_body, 0)


def _attention(proj):
    b, s, _ = proj.shape
    nblk = A_WIDTH // HEAD_DIM

    def col(slot):
        return pl.BlockSpec((None, s, HEAD_DIM), lambda bi, h: (bi, 0, slot * nblk + h))

    return pl.pallas_call(
        _attention_kernel,
        grid=(b, A_HEADS),
        in_specs=[col(0), col(1), col(2), col(3)],
        out_specs=pl.BlockSpec((None, s, HEAD_DIM), lambda bi, h: (bi, 0, h)),
        out_shape=jax.ShapeDtypeStruct((b, s, A_WIDTH), BF16),
        scratch_shapes=[
            pltpu.VMEM((3, s, HEAD_DIM), F32), pltpu.VMEM((3, s, HEAD_DIM), F32),
            pltpu.VMEM((s, HEAD_DIM), BF16), pltpu.VMEM((s, HEAD_DIM), BF16),
            pltpu.VMEM((s, 2 * HEAD_DIM), BF16),
            pltpu.VMEM((s, HEAD_DIM), BF16), pltpu.VMEM((s, HEAD_DIM), BF16),
            pltpu.VMEM((s, 2 * HEAD_DIM), BF16),
            pltpu.VMEM((s, 2 * HEAD_DIM), BF16),
            pltpu.VMEM((3, Q_TILE, K_WIN), F32),
        ],
        compiler_params=pltpu.CompilerParams(
            dimension_semantics=("arbitrary", "arbitrary"),
            vmem_limit_bytes=VMEM_LIMIT),
        name="attention",
    )(proj, proj, proj, proj)


def _ab_out_kernel(ya_ref, ub_ref, gb_ref, gc_ref, zb_ref, ubp_ref, gcp_ref, ubn_ref, gcn_ref,
                   cw_ref, w_ref, x_ref, gate_ref, ng_ref, nshift_ref, nscale_ref,
                   o_ref, *rest, final, row_groups):
    h_ref, pbuf, yb_scr = ((None,) + rest) if final else rest
    i = pl.program_id(1)
    tm = x_ref.shape[0]
    halo = ubp_ref.shape[0]
    chunk = 32

    prev = (gcp_ref[...].astype(F32) * ubp_ref[...].astype(F32))[halo - 1:halo, :]
    nxt = (gcn_ref[...].astype(F32) * ubn_ref[...].astype(F32))[0:1, :]
    pbuf[7:8, :] = jnp.where(i > 0, prev, 0.0)
    pbuf[8 + tm:9 + tm, :] = jnp.where(i < pl.num_programs(1) - 1, nxt, 0.0)
    for c0 in range(0, tm, chunk):
        rows = slice(c0, c0 + chunk)
        pbuf[8 + c0:8 + c0 + chunk, :] = gc_ref[rows, :].astype(F32) * ub_ref[rows, :].astype(F32)
    w0 = cw_ref[0:1, :]
    w1 = cw_ref[1:2, :]
    w2 = cw_ref[2:3, :]

    for d0, d1 in row_groups:
        for c0 in range(d0, d1, chunk):
            rows = slice(c0, c0 + chunk)
            conv = (w0 * pbuf[7 + c0:7 + c0 + chunk, :] + w1 * pbuf[8 + c0:8 + c0 + chunk, :]
                    + w2 * pbuf[9 + c0:9 + c0 + chunk, :])
            gate = gb_ref[rows, :].astype(F32) * zb_ref[rows, :].astype(F32)
            yb_scr[rows, :] = (gate * conv).astype(BF16)
        rows = slice(d0, d1)
        out = jnp.dot(ya_ref[rows, :], w_ref[0:A_WIDTH, :], preferred_element_type=F32)
        out = out + jnp.dot(yb_scr[rows, :], w_ref[A_WIDTH:A_WIDTH + B_WIDTH, :],
                            preferred_element_type=F32)
        xn = x_ref[rows, :] + gate_ref[...] * out
        _residual_tail(xn, rows, ng_ref, nshift_ref, nscale_ref, o_ref, h_ref, final)


def _residual_out(b, s, d, tm, final):
    spec = pl.BlockSpec((None, tm, d), lambda bi, i: (bi, i, 0))
    if final:
        return spec, jax.ShapeDtypeStruct((b, s, d), F32)
    return [spec, spec], [jax.ShapeDtypeStruct((b, s, d), F32),
                          jax.ShapeDtypeStruct((b, s, d), BF16)]


def _ab_out_projection(ya, proj, conv_w, w_bf16, x, gate, next_g, next_shift, next_scale, *,
                       final, tm=512, halo=BF16_ROWS):
    b, s, d = x.shape
    out_specs, out_shape = _residual_out(b, s, d, tm, final)
    row_vec = pl.BlockSpec((None, 1, d), lambda bi, i: (bi, 0, 0))
    per_tile = tm // halo
    n_halo = s // halo

    def slot(k):
        return pl.BlockSpec((None, tm, B_WIDTH), lambda bi, i: (bi, i, k))

    def before(k):
        return pl.BlockSpec((None, halo, B_WIDTH),
                            lambda bi, i: (bi, jnp.maximum(i * per_tile - 1, 0), k))

    def after(k):
        return pl.BlockSpec((None, halo, B_WIDTH),
                            lambda bi, i: (bi, jnp.minimum((i + 1) * per_tile, n_halo - 1), k))

    return pl.pallas_call(
        functools.partial(_ab_out_kernel, final=final, row_groups=_row_groups(tm)),
        grid=(b, s // tm),
        in_specs=[
            pl.BlockSpec((None, tm, A_WIDTH), lambda bi, i: (bi, i, 0)),
            slot(4), slot(5), slot(6), slot(7),
            before(4), before(6), after(4), after(6),
            pl.BlockSpec((B_CONV, B_WIDTH), lambda bi, i: (0, 0)),
            pl.BlockSpec((A_WIDTH + B_WIDTH, d), lambda bi, i: (0, 0),
                         pipeline_mode=pl.Buffered(1)),
            pl.BlockSpec((None, tm, d), lambda bi, i: (bi, i, 0)),
            row_vec,
            pl.BlockSpec((1, d), lambda bi, i: (0, 0)),
            row_vec,
            row_vec,
        ],
        out_specs=out_specs,
        out_shape=out_shape,
        scratch_shapes=[pltpu.VMEM((tm + 16, B_WIDTH), F32), pltpu.VMEM((tm, B_WIDTH), BF16)],
        compiler_params=pltpu.CompilerParams(
            dimension_semantics=("arbitrary", "arbitrary"),
            vmem_limit_bytes=VMEM_LIMIT),
        name="ab_out_projection",
    )(ya, proj, proj, proj, proj, proj, proj, proj, proj, conv_w, w_bf16, x, gate,
      next_g.reshape(1, d), next_shift, next_scale)


def _sgu_kernel(u_ref, vn_ref, z_ref, ws_ref, bs_ref, w_ref, x_ref, gate_ref,
                ng_ref, nshift_ref, nscale_ref, o_ref, *rest, final, row_groups):
    h_ref, y_scr = (None, rest[0]) if final else rest
    tm = u_ref.shape[0]
    gw = C_WIDTH // C_GROUPS
    chunks = range(0, tm, C_CHUNK)
    for g in range(C_GROUPS):
        cols = slice(g * gw, (g + 1) * gw)
        vn = jnp.concatenate([vn_ref[c0:c0 + C_CHUNK, cols] for c0 in chunks], axis=1)
        mixed = jnp.dot(ws_ref[g], vn, preferred_element_type=F32) + bs_ref[:, g:g + 1]
        for k, c0 in enumerate(chunks):
            rows = slice(c0, c0 + C_CHUNK)
            gated = u_ref[rows, cols].astype(F32) * z_ref[rows, cols].astype(F32)
            y_scr[rows, cols] = (gated * mixed[:, k * gw:(k + 1) * gw]).astype(BF16)
    for d0, d1 in row_groups:
        rows = slice(d0, d1)
        out = jnp.dot(y_scr[rows, :], w_ref[...], preferred_element_type=F32)
        xn = x_ref[rows, :] + gate_ref[...] * out
        _residual_tail(xn, rows, ng_ref, nshift_ref, nscale_ref, o_ref, h_ref, final)


def _sgu(proj, ws_bf16, bs_t, w_bf16, x, gate, next_g, next_shift, next_scale, *,
         final, tm=512):
    b, s, d = x.shape
    cw = C_WIDTH
    kern = functools.partial(_sgu_kernel, final=final, row_groups=_row_groups(tm))
    out_specs, out_shape = _residual_out(b, s, d, tm, final)
    row_vec = pl.BlockSpec((None, 1, d), lambda bi, i: (bi, 0, 0))
    return pl.pallas_call(
        kern,
        grid=(b, s // tm),
        in_specs=[
            pl.BlockSpec((None, tm, cw), lambda bi, i: (bi, i, 0)),
            pl.BlockSpec((None, tm, cw), lambda bi, i: (bi, i, 1)),
            pl.BlockSpec((None, tm, cw), lambda bi, i: (bi, i, 2)),
            pl.BlockSpec((C_GROUPS, C_CHUNK, C_CHUNK), lambda bi, i: (0, 0, 0)),
            pl.BlockSpec((C_CHUNK, C_GROUPS), lambda bi, i: (0, 0)),
            pl.BlockSpec((cw, d), lambda bi, i: (0, 0), pipeline_mode=pl.Buffered(1)),
            pl.BlockSpec((None, tm, d), lambda bi, i: (bi, i, 0)),
            row_vec,
            pl.BlockSpec((1, d), lambda bi, i: (0, 0)),
            row_vec,
            row_vec,
        ],
        out_specs=out_specs,
        out_shape=out_shape,
        scratch_shapes=[pltpu.VMEM((tm, cw), BF16)],
        compiler_params=pltpu.CompilerParams(
            dimension_semantics=("arbitrary", "arbitrary"),
            vmem_limit_bytes=VMEM_LIMIT),
        name="sgu_final" if final else "sgu",
    )(proj, proj, proj, ws_bf16, bs_t, w_bf16, x, gate, next_g.reshape(1, d), next_shift,
      next_scale)


def _rope_tables(s):
    half = HEAD_DIM // 2
    inv = np.float32(ROPE_THETA) ** (-np.arange(half, dtype=np.float32) / np.float32(half))
    ang = np.arange(s, dtype=np.float32)[:, None] * inv[None, :].astype(np.float32)
    cos = np.cos(ang).astype(np.float32)
    sin = np.sin(ang).astype(np.float32)
    return (jnp.asarray(np.concatenate([cos, cos], axis=-1)),
            jnp.asarray(np.concatenate([-sin, sin], axis=-1)))


def _split_mod(mod, batch):
    m = mod[:batch].reshape(batch, 1, 3, D_MODEL)
    return m[:, :, 0, :], m[:, :, 1, :], m[:, :, 2, :]


def kernel(x, c, ab_norm_g, ab_w_mod, ab_b_mod, ab_w_in, ab_conv_w, ab_w_out, sg_norm_g, sg_w_mod, sg_b_mod, sg_w_in, sg_ln_g, sg_ln_b, sg_w_s, sg_b_s, sg_w_out, final_norm_g):
    batch, s, _ = x.shape
    depth = ab_w_in.shape[0] + sg_w_in.shape[0]
    c_pad = jnp.pad(c, ((0, BF16_ROWS - batch), (0, 0)))
    cos, sin = _rope_tables(s)

    w_in_b = ab_w_in[0].astype(BF16)
    sg_w_s_b = sg_w_s.astype(BF16)

    def layer_params(layer, mod_vec):
        norm_g = (ab_norm_g, sg_norm_g)[layer % 2][layer // 2]
        return (norm_g,) + _split_mod(mod_vec, batch)

    def mod_weights(layer):
        w_mod, b_mod = ((ab_w_mod, ab_b_mod), (sg_w_mod, sg_b_mod))[layer % 2]
        return w_mod, b_mod, layer // 2

    params = layer_params(0, _modulation(c_pad, *mod_weights(0)))

    h = None
    for layer in range(depth):
        i = layer // 2
        norm_g, shift, scale, gate = params
        final = layer == depth - 1
        mod = None if final else (c_pad,) + mod_weights(layer + 1)
        src, normalize = (x, True) if h is None else (h, False)
        if layer % 2 == 0:
            casts = ((ab_w_out, i),) + (() if final else ((sg_w_in, i),))
            proj, w_out_b, *more = _in_projection(src, norm_g, shift, scale, w_in_b, cos, sin,
                                                  epilogues=AB_EPILOGUES, normalize=normalize,
                                                  casts=casts, mod=mod)
        else:
            casts = ((sg_w_out, i),) + (() if final else ((ab_w_in, i + 1),))
            proj, w_out_b, *more = _in_projection(src, norm_g, shift, scale, w_in_b, cos, sin,
                                                  epilogues=SG_EPILOGUES, normalize=normalize,
                                                  casts=casts, mod=mod,
                                                  ln=(sg_ln_g[i], sg_ln_b[i]))
        if final:
            nxt = (final_norm_g, gate, gate)
        else:
            w_in_b, mod_vec = more
            params = layer_params(layer + 1, mod_vec)
            nxt = params[:3]
        if layer % 2 == 0:
            res = _ab_out_projection(_attention(proj), proj, ab_conv_w[i], w_out_b, x, gate, *nxt,
                                     final=final)
        else:
            res = _sgu(proj, sg_w_s_b[i], sg_b_s[i].T, w_out_b, x, gate, *nxt, final=final)
        if final:
            return res
        x, h = res
```
